```python
import jax, jax.numpy as jnp
from jax import lax
import numpy as np

D_MODEL = 1024
BATCH = 4
SEQ = 4096
DEPTH = 2

CTX_LEN = 256
GRID_W = 64
POS_BASE = 10000.0
D_FOURIER = D_MODEL // 2
FOURIER_GROUPS = 4
D_LRU = D_MODEL // 2
LRU_HEADS = 8
LRU_HEAD_DIM = D_LRU // LRU_HEADS
CONV_W = 4
LRU_C = 8.0
D_IN = D_FOURIER + 2 * D_LRU
D_MIX = D_FOURIER + D_LRU
N_EXPERTS = 16
EC_CAPACITY = 2
D_FF = 2816
N_MOD = 6
EPS = 1e-6

kernel_name = "hybrid_fourier_rglru_ec_moe_dit"


def rmsnorm(x, g):
    xf = x.astype(jnp.float32)
    y = xf * lax.rsqrt(jnp.mean(xf * xf, axis=-1, keepdims=True) + EPS) * g.astype(jnp.float32)
    return y.astype(x.dtype)


def modulate(h, shift, scale):
    return h * (1 + scale) + shift


def grid_pos_embed(rows, d):
    rr, cc = jnp.meshgrid(jnp.arange(rows, dtype=jnp.float32), jnp.arange(GRID_W, dtype=jnp.float32), indexing="ij")
    quarter = d // 4
    omega = 1.0 / (POS_BASE ** (jnp.arange(quarter, dtype=jnp.float32) / quarter))

    def enc(p):
        ang = p.reshape(-1)[:, None] * omega
        return jnp.concatenate([jnp.sin(ang), jnp.cos(ang)], axis=-1)

    return jnp.concatenate([enc(rr), enc(cc)], axis=-1)


def fourier_mix(u):
    b, n, _ = u.shape
    ug = u.astype(jnp.float32).reshape(b, n, FOURIER_GROUPS, D_FOURIER // FOURIER_GROUPS)
    y = jnp.fft.fft2(ug, axes=(1, 3), norm="ortho").real
    return y.reshape(b, n, D_FOURIER).astype(u.dtype)


def centred_dwconv(u, w, bias):
    n = u.shape[1]
    left = (CONV_W - 1) // 2
    up = jnp.pad(u, ((0, 0), (left, CONV_W - 1 - left), (0, 0)))
    y = up[:, 0:n] * w[0]
    for k in range(1, CONV_W):
        y = y + up[:, k:k + n] * w[k]
    return y + bias


def _combine(left, right):
    a_l, b_l = left
    a_r, b_r = right
    return a_l * a_r, a_r * b_l + b_r


def linear_scan(a, b, h0, reverse):
    if reverse:
        a = jnp.flip(a, axis=1)
        b = jnp.flip(b, axis=1)
    if h0 is not None:
        b = b.at[:, 0].add(a[:, 0] * h0)
    _, h = lax.associative_scan(_combine, (a, b), axis=1)
    return jnp.flip(h, axis=1) if reverse else h


def bidir_rglru(u, conv_w, conv_b, wa, ba, wi, bi, lam, h0):
    xc = centred_dwconv(u, conv_w, conv_b).astype(jnp.float32)
    bsz, n, _ = xc.shape
    xh = xc.reshape(bsz, n, LRU_HEADS, LRU_HEAD_DIM)
    y = jnp.zeros_like(xc)
    finals = []
    for dirn in range(2):
        r = jax.nn.sigmoid(jnp.einsum("bnhi,hij->bnhj", xh, wa[dirn].astype(jnp.float32)).reshape(bsz, n, D_LRU)
                           + ba[dirn].astype(jnp.float32))
        gi = jax.nn.sigmoid(jnp.einsum("bnhi,hij->bnhj", xh, wi[dirn].astype(jnp.float32)).reshape(bsz, n, D_LRU)
                            + bi[dirn].astype(jnp.float32))
        log_a = -LRU_C * r * jax.nn.softplus(-lam[dirn].astype(jnp.float32))
        a = jnp.exp(log_a)
        drive = jnp.sqrt(-jnp.expm1(2.0 * log_a)) * (gi * xc)
        init = None if h0 is None else h0[dirn]
        h = linear_scan(a, drive, init, reverse=(dirn == 1))
        y = y + h
        finals.append(h[:, -1] if dirn == 0 else h[:, 0])
    return y, (finals[0], finals[1])


def token_mixer(h, w_in, conv_w, conv_b, wa, ba, wi, bi, lam, w_out, h0):
    proj = h @ w_in
    u_f = proj[..., :D_FOURIER]
    u_x = proj[..., D_FOURIER:D_FOURIER + D_LRU]
    u_g = proj[..., D_FOURIER + D_LRU:]
    y_f = fourier_mix(u_f)
    y_r, finals = bidir_rglru(u_x, conv_w, conv_b, wa, ba, wi, bi, lam, h0)
    y_r = y_r.astype(h.dtype) * jax.nn.gelu(u_g)
    return jnp.concatenate([y_f, y_r], axis=-1) @ w_out, finals


def expert_choice_ffn(h, w_router, w_gate, w_up, w_down):
    b, n, _ = h.shape
    cap = EC_CAPACITY * n // N_EXPERTS
    aff = jax.nn.softmax(jnp.einsum("bnd,de->bne", h.astype(jnp.float32), w_router.astype(jnp.float32)), axis=-1)
    g, idx = lax.top_k(jnp.swapaxes(aff, 1, 2), cap)
    bidx = jnp.arange(b)[:, None, None]
    xg = h[bidx, idx]
    hid = jax.nn.silu(jnp.einsum("becd,edf->becf", xg, w_gate)) * jnp.einsum("becd,edf->becf", xg, w_up)
    y = jnp.einsum("becf,efd->becd", hid, w_down) * g[..., None].astype(h.dtype)
    return jnp.zeros_like(h).at[bidx, idx].add(y)


def setup_inputs(seed: int = 0) -> dict:
    key = jax.random.key(seed)
    k = jax.random.split(key, 24)
    f32 = jnp.float32
    nrm = lambda kk, shape, s: jax.random.normal(kk, shape, f32) * s
    u = jax.random.uniform(k[16], (DEPTH, 2, D_LRU), f32, minval=0.9, maxval=0.999)
    a0 = u ** (1.0 / LRU_C)
    return {
        "x": nrm(k[0], (BATCH, SEQ, D_MODEL), 1.0),
        "c": nrm(k[1], (BATCH, D_MODEL), 1.0),
        "ctx": nrm(k[2], (BATCH, CTX_LEN, D_MODEL), 1.0),
        "c_ctx": nrm(k[3], (D_MODEL,), 1.0),
        "w_mod": nrm(k[4], (DEPTH, D_MODEL, N_MOD * D_MODEL), 0.5 * D_MODEL ** -0.5),
        "b_mod": nrm(k[5], (DEPTH, N_MOD * D_MODEL), 0.02),
        "norm1_g": 1.0 + nrm(k[6], (DEPTH, D_MODEL), 0.02),
        "norm2_g": 1.0 + nrm(k[7], (DEPTH, D_MODEL), 0.02),
        "w_in": nrm(k[8], (DEPTH, D_MODEL, D_IN), D_MODEL ** -0.5),
        "conv_w": nrm(k[9], (DEPTH, CONV_W, D_LRU), CONV_W ** -0.5),
        "conv_b": nrm(k[10], (DEPTH, D_LRU), 0.02),
        "lru_wa": nrm(k[11], (DEPTH, 2, LRU_HEADS, LRU_HEAD_DIM, LRU_HEAD_DIM), LRU_HEAD_DIM ** -0.5),
        "lru_ba": nrm(k[12], (DEPTH, 2, D_LRU), 0.02),
        "lru_wi": nrm(k[13], (DEPTH, 2, LRU_HEADS, LRU_HEAD_DIM, LRU_HEAD_DIM), LRU_HEAD_DIM ** -0.5),
        "lru_bi": nrm(k[14], (DEPTH, 2, D_LRU), 0.02),
        "lru_lambda": jnp.log(a0) - jnp.log1p(-a0),
        "w_out": nrm(k[15], (DEPTH, D_MIX, D_MODEL), D_MIX ** -0.5),
        "w_router": nrm(k[17], (DEPTH, D_MODEL, N_EXPERTS), D_MODEL ** -0.5),
        "w_gate": nrm(k[18], (DEPTH, N_EXPERTS, D_MODEL, D_FF), D_MODEL ** -0.5),
        "w_up": nrm(k[19], (DEPTH, N_EXPERTS, D_MODEL, D_FF), D_MODEL ** -0.5),
        "w_down": nrm(k[20], (DEPTH, N_EXPERTS, D_FF, D_MODEL), D_FF ** -0.5),
        "final_g": 1.0 + nrm(k[21], (D_MODEL,), 0.02),
    }


def reference(x, c, ctx, c_ctx, w_mod, b_mod, norm1_g, norm2_g, w_in, conv_w, conv_b,
              lru_wa, lru_ba, lru_wi, lru_bi, lru_lambda, w_out, w_router, w_gate, w_up, w_down, final_g):
    ROWS = x.shape[1] // GRID_W
    x = x + grid_pos_embed(ROWS, x.shape[-1]).astype(x.dtype)[None]
    cond = jax.nn.silu(c)
    cond_ctx = jax.nn.silu(c_ctx)
    for l in range(DEPTH):
        last = l == DEPTH - 1
        mod = cond @ w_mod[l] + b_mod[l]
        sh1, sc1, g1, sh2, sc2, g2 = jnp.split(mod[:, None, :], N_MOD, axis=-1)
        mc = cond_ctx @ w_mod[l] + b_mod[l]
        csh1, csc1, cg1, csh2, csc2, cg2 = jnp.split(mc, N_MOD, axis=-1)
        lru_p = (conv_w[l], conv_b[l], lru_wa[l], lru_ba[l], lru_wi[l], lru_bi[l], lru_lambda[l])

        hc = modulate(rmsnorm(ctx, norm1_g[l]), csh1, csc1)
        if last:
            u_cx = hc @ w_in[l][:, D_FOURIER:D_FOURIER + D_LRU]
            _, ctx_states = bidir_rglru(u_cx, *lru_p, None)
        else:
            ctx_mix, ctx_states = token_mixer(hc, w_in[l], *lru_p, w_out[l], None)
            ctx = ctx + cg1 * ctx_mix
            hc2 = modulate(rmsnorm(ctx, norm2_g[l]), csh2, csc2)
            ctx = ctx + cg2 * expert_choice_ffn(hc2, w_router[l], w_gate[l], w_up[l], w_down[l])

        hx = modulate(rmsnorm(x, norm1_g[l]), sh1, sc1)
        x_mix, _ = token_mixer(hx, w_in[l], *lru_p, w_out[l], ctx_states)
        x = x + g1 * x_mix
        hx2 = modulate(rmsnorm(x, norm2_g[l]), sh2, sc2)
        x = x + g2 * expert_choice_ffn(hx2, w_router[l], w_gate[l], w_up[l], w_down[l])
    return rmsnorm(x, final_g)
```

```python
import functools
import math

import numpy as np
import jax
import jax.numpy as jnp
from jax import lax
from jax.experimental import pallas as pl
from jax.experimental.pallas import tpu as pltpu

D_MODEL = 1024
GRID_W = 64
POS_BASE = 10000.0
D_FOURIER = 512
FOURIER_GROUPS = 4
D_GROUP = D_FOURIER // FOURIER_GROUPS
D_LRU = 512
LRU_HEADS = 8
LRU_HEAD_DIM = D_LRU // LRU_HEADS
CONV_W = 4
LRU_C = 8.0
D_IN = D_FOURIER + 2 * D_LRU
N_EXPERTS = 16
EC_CAPACITY = 2
D_FF = 2816
N_MOD = 6
EPS = 1e-6

LANES = 128
SUBLANES = 8
DFT_BLOCK = 256
VMEM_LIMIT = 56 * 1024 * 1024

F32 = jnp.float32
BF16 = jnp.bfloat16


def _cparams(sem, **kw):
    return pltpu.CompilerParams(dimension_semantics=sem, vmem_limit_bytes=VMEM_LIMIT, **kw)


def _mod_kernel(c_ref, w_ref, b_ref, o_ref):
    c = c_ref[...]
    cond = c * jax.nn.sigmoid(c)
    o_ref[...] = jnp.dot(cond, w_ref[...], preferred_element_type=F32,
                         precision=lax.Precision.HIGHEST) + b_ref[...]


def _mod_call(crows, w_mod, b_mod):
    depth, d, nm = w_mod.shape
    tn = 1536
    return pl.pallas_call(
        _mod_kernel,
        grid=(depth, nm // tn),
        in_specs=[
            pl.BlockSpec((SUBLANES, d), lambda l, j: (0, 0)),
            pl.BlockSpec((None, d, tn), lambda l, j: (l, 0, j)),
            pl.BlockSpec((None, 1, tn), lambda l, j: (l, 0, j)),
        ],
        out_specs=pl.BlockSpec((None, SUBLANES, tn), lambda l, j: (l, 0, j)),
        out_shape=jax.ShapeDtypeStruct((depth, SUBLANES, nm), F32),
        compiler_params=_cparams(("arbitrary", "arbitrary")),
        name="mod",
    )(crows, w_mod, b_mod.reshape(depth, 1, nm))


def _rms_mod(x, g, shift, scale):
    y = x * lax.rsqrt(jnp.mean(x * x, axis=-1, keepdims=True) + EPS) * g
    return y * (1.0 + scale) + shift


def _proj_kernel(add_pos, *refs):
    if add_pos:
        x_ref, pos_ref, g_ref, sh_ref, sc_ref, w_ref, uf_ref, ux_ref, ug_ref, xp_ref, wbf_ref = refs
        x = x_ref[...] + pos_ref[...]
        xp_ref[...] = x
    else:
        x_ref, g_ref, sh_ref, sc_ref, w_ref, uf_ref, ux_ref, ug_ref, wbf_ref = refs
        x = x_ref[...]

    @pl.when((pl.program_id(0) == 0) & (pl.program_id(1) == 0))
    def _cast_w():
        wbf_ref[...] = w_ref[...].astype(BF16)

    h = _rms_mod(x, g_ref[...], sh_ref[...], sc_ref[...])
    u = jnp.dot(h.astype(BF16), wbf_ref[...], preferred_element_type=F32)
    uf_ref[...] = u[:, :D_FOURIER]
    ux_ref[...] = u[:, D_FOURIER:D_FOURIER + D_LRU]
    ug_ref[...] = u[:, D_FOURIER + D_LRU:]


def _proj_call(x, pos, g, shift, scale, w_in_bf):
    b, n, d = x.shape
    tm = min(512, n)
    add_pos = pos is not None
    row = pl.BlockSpec((None, tm, d), lambda i, j: (i, j, 0))
    vec_b = pl.BlockSpec((None, 1, d), lambda i, j: (i, 0, 0))
    in_specs = [row]
    args = [x]
    if add_pos:
        in_specs.append(pl.BlockSpec((tm, d), lambda i, j: (j, 0)))
        args.append(pos)
    in_specs += [pl.BlockSpec((1, d), lambda i, j: (0, 0)), vec_b, vec_b,
                 pl.BlockSpec((d, D_IN), lambda i, j: (0, 0))]
    args += [g.reshape(1, d), shift, scale, w_in_bf]
    part = pl.BlockSpec((None, tm, D_FOURIER), lambda i, j: (i, j, 0))
    out_specs = [part, part, part]
    out_shape = [jax.ShapeDtypeStruct((b, n, D_FOURIER), F32)] * 3
    if add_pos:
        out_specs.append(row)
        out_shape.append(jax.ShapeDtypeStruct((b, n, d), F32))
    return pl.pallas_call(
        functools.partial(_proj_kernel, add_pos),
        grid=(b, n // tm),
        in_specs=in_specs,
        out_specs=out_specs,
        out_shape=out_shape,
        scratch_shapes=[pltpu.VMEM((d, D_IN), BF16)],
        compiler_params=_cparams(("arbitrary", "arbitrary")),
        name="proj",
    )(*args)


def _cmul_const(re, im, c, s):
    tol = 1e-12
    if abs(s) < tol:
        if abs(c - 1.0) < tol:
            return re, im
        if abs(c + 1.0) < tol:
            return -re, -im
        return re * c, im * c
    if abs(c) < tol:
        if abs(s - 1.0) < tol:
            return -im, re
        if abs(s + 1.0) < tol:
            return im, -re
        return -im * s, re * s
    return re * c - im * s, re * s + im * c


def _fft_list(xs):
    n = len(xs)
    if n == 1:
        return xs
    ev = _fft_list(xs[0::2])
    od = _fft_list(xs[1::2])
    out = [None] * n
    for k in range(n // 2):
        ang = -2.0 * math.pi * k / n
        tr, ti = _cmul_const(od[k][0], od[k][1], math.cos(ang), math.sin(ang))
        out[k] = (ev[k][0] + tr, ev[k][1] + ti)
        out[k + n // 2] = (ev[k][0] - tr, ev[k][1] - ti)
    return out


def _fourier_kernel(n1_count, u_ref, cs_ref, twc_ref, tws_ref, cc_ref, o_ref, tr_ref, ti_ref, pq_ref):
    n = u_ref.shape[0]
    cs = cs_ref[...].astype(BF16)
    for n1 in range(n1_count):
        if n1_count == 1:
            z = u_ref[...]
        else:
            z = u_ref[pl.ds(n1, DFT_BLOCK, stride=n1_count), :]
        g = jnp.dot(cs, z.astype(BF16), preferred_element_type=F32)
        gr = g[:DFT_BLOCK]
        gs = g[DFT_BLOCK:]
        if n1 == 0:
            tr_ref[n1] = gr
            ti_ref[n1] = -gs
        else:
            c = twc_ref[n1]
            s = tws_ref[n1]
            tr_ref[n1] = gr * c - gs * s
            ti_ref[n1] = -(gr * s + gs * c)

    if n1_count == 1:
        pq_ref[:, :D_GROUP] = tr_ref[0]
        pq_ref[:, D_GROUP:] = -ti_ref[0]
    else:
        def chunk(j, carry):
            r0 = pl.multiple_of(j * SUBLANES, SUBLANES)
            xs = [(tr_ref[i, pl.ds(r0, SUBLANES), :], ti_ref[i, pl.ds(r0, SUBLANES), :])
                  for i in range(n1_count)]
            ys = _fft_list(xs)
            for k1 in range(n1_count):
                rows = pl.ds(pl.multiple_of(k1 * DFT_BLOCK + r0, SUBLANES), SUBLANES)
                pq_ref[rows, :D_GROUP] = ys[k1][0]
                pq_ref[rows, D_GROUP:] = -ys[k1][1]
            return carry
        lax.fori_loop(0, DFT_BLOCK // SUBLANES, chunk, 0)

    cc = cc_ref[...].astype(BF16)
    tile = min(512, n)
    for i in range(n // tile):
        rows = pl.ds(i * tile, tile)
        o_ref[rows, :] = jnp.dot(pq_ref[rows, :].astype(BF16), cc, preferred_element_type=F32)


@functools.lru_cache(maxsize=None)
def _fourier_consts(n):
    n1c = n // DFT_BLOCK
    k = np.arange(DFT_BLOCK, dtype=np.float64)
    ang = 2.0 * np.pi * np.outer(k, k) / DFT_BLOCK
    cs = np.concatenate([np.cos(ang), np.sin(ang)], axis=0)
    tw = 2.0 * np.pi * np.outer(np.arange(n1c, dtype=np.float64), k) / n
    twc = np.repeat(np.cos(tw)[:, :, None], LANES, axis=2)
    tws = np.repeat(np.sin(tw)[:, :, None], LANES, axis=2)
    c = np.arange(D_GROUP, dtype=np.float64)
    angc = 2.0 * np.pi * np.outer(c, c) / D_GROUP
    scale = 1.0 / math.sqrt(n * D_GROUP)
    cc = np.concatenate([np.cos(angc), -np.sin(angc)], axis=0) * scale
    return (cs.astype(np.float32), twc.astype(np.float32), tws.astype(np.float32), cc.astype(np.float32))


def _fourier_call(u):
    b, n, _ = u.shape
    n1c = n // DFT_BLOCK
    cs, twc, tws, cc = _fourier_consts(n)
    blk = pl.BlockSpec((None, n, D_GROUP), lambda i, j: (i, 0, j))
    return pl.pallas_call(
        functools.partial(_fourier_kernel, n1c),
        grid=(b, FOURIER_GROUPS),
        in_specs=[
            blk,
            pl.BlockSpec((2 * DFT_BLOCK, DFT_BLOCK), lambda i, j: (0, 0)),
            pl.BlockSpec((n1c, DFT_BLOCK, LANES), lambda i, j: (0, 0, 0)),
            pl.BlockSpec((n1c, DFT_BLOCK, LANES), lambda i, j: (0, 0, 0)),
            pl.BlockSpec((2 * D_GROUP, D_GROUP), lambda i, j: (0, 0)),
        ],
        out_specs=blk,
        out_shape=jax.ShapeDtypeStruct(u.shape, F32),
        scratch_shapes=[
            pltpu.VMEM((n1c, DFT_BLOCK, D_GROUP), F32),
            pltpu.VMEM((n1c, DFT_BLOCK, D_GROUP), F32),
            pltpu.VMEM((n, 2 * D_GROUP), F32),
        ],
        compiler_params=_cparams(("arbitrary", "arbitrary")),
        name="fourier",
    )(u, jnp.asarray(cs), jnp.asarray(twc), jnp.asarray(tws), jnp.asarray(cc))


SEGMENTS = SUBLANES
PAD_ROWS = 8


def _expm1(x):
    u = jnp.exp(x)
    near = (u - 1.0) * x / jnp.log(jnp.where(u == 1.0, 2.0, u))
    return jnp.where(x < -0.5, u - 1.0, jnp.where(u == 1.0, x, near))


def _lru_kernel(has_h0, *refs):
    if has_h0:
        (ux_ref, ug_ref, cw_ref, cb_ref, wg_ref, bg_ref, lam_ref, h0_ref,
         y_ref, fin_ref, upad_ref, a_ref, b_ref) = refs
    else:
        (ux_ref, ug_ref, cw_ref, cb_ref, wg_ref, bg_ref, lam_ref,
         y_ref, fin_ref, upad_ref, a_ref, b_ref) = refs
    n = ux_ref.shape[0]
    seg_len = n // SEGMENTS
    tile = min(256, seg_len)

    zeros_pad = jnp.zeros((PAD_ROWS, LANES), F32)
    upad_ref[pl.ds(0, PAD_ROWS), :] = zeros_pad
    upad_ref[pl.ds(PAD_ROWS + n, PAD_ROWS), :] = zeros_pad
    upad_ref[pl.ds(PAD_ROWS, n), :] = ux_ref[...]

    lam = lam_ref[...]
    sp = jnp.maximum(-lam, 0.0) + jnp.log1p(jnp.exp(-jnp.abs(lam)))
    cw = cw_ref[...]
    cb = cb_ref[...]
    wg = wg_ref[...].astype(BF16)
    bg = bg_ref[...]
    left = (CONV_W - 1) // 2

    for t in range(n // tile):
        r0 = t * tile
        xc = cb
        for k in range(CONV_W):
            xc = xc + upad_ref[pl.ds(PAD_ROWS + r0 + k - left, tile), :] * cw[k:k + 1, :]
        gates = jnp.dot(xc.astype(BF16), wg, preferred_element_type=F32) + bg
        seg = r0 // seg_len
        s0 = r0 - seg * seg_len
        for d in range(2):
            r = jax.nn.sigmoid(gates[:, (2 * d) * LANES:(2 * d + 1) * LANES])
            gi = jax.nn.sigmoid(gates[:, (2 * d + 1) * LANES:(2 * d + 2) * LANES])
            log_a = (-LRU_C) * r * sp[d:d + 1, :]
            a = jnp.exp(log_a)
            drive = jnp.sqrt(-_expm1(2.0 * log_a)) * (gi * xc)
            dst = pl.ds(s0 * SEGMENTS + seg, tile, stride=SEGMENTS)
            a_ref[d, dst, :] = a
            b_ref[d, dst, :] = drive

    def rows(s):
        return pl.ds(pl.multiple_of(s * SEGMENTS, SEGMENTS), SEGMENTS)

    def pass1(t, carry):
        hf, af, hb, ab = carry
        sb = seg_len - 1 - t
        a0 = a_ref[0, rows(t), :]
        a1 = a_ref[1, rows(sb), :]
        hf = a0 * hf + b_ref[0, rows(t), :]
        hb = a1 * hb + b_ref[1, rows(sb), :]
        return hf, af * a0, hb, ab * a1

    z = jnp.zeros((SEGMENTS, LANES), F32)
    o = jnp.ones((SEGMENTS, LANES), F32)
    hf, af, hb, ab = lax.fori_loop(0, seg_len, pass1, (z, o, z, o), unroll=8)

    if has_h0:
        cf = h0_ref[0:1, :]
        cbk = h0_ref[1:2, :]
    else:
        cf = jnp.zeros((1, LANES), F32)
        cbk = jnp.zeros((1, LANES), F32)
    sub = lax.broadcasted_iota(jnp.int32, (SEGMENTS, LANES), 0)
    carry_f = z
    for j in range(SEGMENTS):
        carry_f = jnp.where(sub == j, cf, carry_f)
        cf = hf[j:j + 1, :] + af[j:j + 1, :] * cf
    carry_b = z
    for j in reversed(range(SEGMENTS)):
        carry_b = jnp.where(sub == j, cbk, carry_b)
        cbk = hb[j:j + 1, :] + ab[j:j + 1, :] * cbk
    fin_ref[0:1, :] = cf
    fin_ref[1:2, :] = cbk

    def pass2(t, carry):
        hf, hb = carry
        sb = seg_len - 1 - t
        hf = a_ref[0, rows(t), :] * hf + b_ref[0, rows(t), :]
        b_ref[0, rows(t), :] = hf
        hb = a_ref[1, rows(sb), :] * hb + b_ref[1, rows(sb), :]
        b_ref[1, rows(sb), :] = hb
        return hf, hb

    lax.fori_loop(0, seg_len, pass2, (carry_f, carry_b), unroll=8)

    for t in range(n // tile):
        r0 = t * tile
        seg = r0 // seg_len
        s0 = r0 - seg * seg_len
        src = pl.ds(s0 * SEGMENTS + seg, tile, stride=SEGMENTS)
        hsum = b_ref[0, src, :] + b_ref[1, src, :]
        y_ref[pl.ds(r0, tile), :] = hsum * jax.nn.gelu(ug_ref[pl.ds(r0, tile), :])


def _lru_call(ux, ug, lp, h0):
    b, n, _ = ux.shape
    nblk = D_LRU // LANES
    has_h0 = h0 is not None
    blk = pl.BlockSpec((None, n, LANES), lambda i, j: (i, 0, j))
    in_specs = [
        blk, blk,
        pl.BlockSpec((CONV_W, LANES), lambda i, j: (0, j)),
        pl.BlockSpec((1, LANES), lambda i, j: (0, j)),
        pl.BlockSpec((None, LANES, 4 * LANES), lambda i, j: (j, 0, 0)),
        pl.BlockSpec((None, 1, 4 * LANES), lambda i, j: (j, 0, 0)),
        pl.BlockSpec((2, LANES), lambda i, j: (0, j)),
    ]
    args = [ux, ug, lp["conv_w"], lp["conv_b"], lp["w_gates"], lp["b_gates"], lp["lam"]]
    if has_h0:
        in_specs.append(pl.BlockSpec((None, 2, LANES), lambda i, j: (i, 0, j)))
        args.append(h0)
    return pl.pallas_call(
        functools.partial(_lru_kernel, has_h0),
        grid=(b, nblk),
        in_specs=in_specs,
        out_specs=[blk, pl.BlockSpec((None, 2, LANES), lambda i, j: (i, 0, j))],
        out_shape=[jax.ShapeDtypeStruct((b, n, D_LRU), F32),
                   jax.ShapeDtypeStruct((b, 2, D_LRU), F32)],
        scratch_shapes=[
            pltpu.VMEM((n + 2 * PAD_ROWS, LANES), F32),
            pltpu.VMEM((2, n, LANES), F32),
            pltpu.VMEM((2, n, LANES), F32),
        ],
        compiler_params=_cparams(("arbitrary", "arbitrary")),
        name="lru",
    )(*args)


def _lru_params(conv_w, conv_b, wa, ba, wi, bi, lam):
    nblk = D_LRU // LANES
    hpb = LANES // LRU_HEAD_DIM

    def blockdiag(w):
        w = w.reshape(nblk, hpb, LRU_HEAD_DIM, LRU_HEAD_DIM)
        eye = jnp.eye(hpb, dtype=w.dtype)
        return jnp.einsum("bhij,hg->bhigj", w, eye).reshape(nblk, LANES, LANES)

    w_gates = jnp.concatenate([blockdiag(wa[0]), blockdiag(wi[0]), blockdiag(wa[1]), blockdiag(wi[1])], axis=-1)
    b_gates = jnp.stack([ba[0], bi[0], ba[1], bi[1]], axis=0)
    b_gates = b_gates.reshape(4, nblk, LANES).transpose(1, 0, 2).reshape(nblk, 1, 4 * LANES)
    return {"conv_w": conv_w, "conv_b": conv_b.reshape(1, D_LRU), "w_gates": w_gates,
            "b_gates": b_gates, "lam": lam}


def _mixout_kernel(x_ref, yf_ref, yr_ref, w_ref, g1_ref, n2_ref, sh_ref, sc_ref, wr_ref,
                   xo_ref, h2_ref, lg_ref, wbf_ref):
    @pl.when((pl.program_id(0) == 0) & (pl.program_id(1) == 0))
    def _cast_w():
        wbf_ref[...] = w_ref[...].astype(BF16)

    mix = jnp.dot(yf_ref[...].astype(BF16), wbf_ref[:D_FOURIER, :], preferred_element_type=F32)
    mix = mix + jnp.dot(yr_ref[...].astype(BF16), wbf_ref[D_FOURIER:, :], preferred_element_type=F32)
    x = x_ref[...] + g1_ref[...] * mix
    xo_ref[...] = x
    h2 = _rms_mod(x, n2_ref[...], sh_ref[...], sc_ref[...])
    h2_ref[...] = h2
    lg_ref[...] = lax.dot_general(wr_ref[...], h2, (((1,), (1,)), ((), ())),
                                  preferred_element_type=F32, precision=lax.Precision.HIGHEST)


def _mixout_call(x, yf, yr, w_out_bf, g1, n2g, sh2, sc2, w_router_t):
    b, n, d = x.shape
    tm = min(512, n)
    row = pl.BlockSpec((None, tm, d), lambda i, j: (i, j, 0))
    half = pl.BlockSpec((None, tm, D_FOURIER), lambda i, j: (i, j, 0))
    vec_b = pl.BlockSpec((None, 1, d), lambda i, j: (i, 0, 0))
    return pl.pallas_call(
        _mixout_kernel,
        grid=(b, n // tm),
        in_specs=[row, half, half,
                  pl.BlockSpec((d, d), lambda i, j: (0, 0)),
                  vec_b,
                  pl.BlockSpec((1, d), lambda i, j: (0, 0)),
                  vec_b, vec_b,
                  pl.BlockSpec((N_EXPERTS, d), lambda i, j: (0, 0))],
        out_specs=[row, row, pl.BlockSpec((None, N_EXPERTS, tm), lambda i, j: (i, 0, j))],
        out_shape=[jax.ShapeDtypeStruct((b, n, d), F32),
                   jax.ShapeDtypeStruct((b, n, d), F32),
                   jax.ShapeDtypeStruct((b, N_EXPERTS, n), F32)],
        scratch_shapes=[pltpu.VMEM((d, d), BF16)],
        compiler_params=_cparams(("arbitrary", "arbitrary")),
        name="mixout",
    )(x, yf, yr, w_out_bf, g1, n2g.reshape(1, d), sh2, sc2, w_router_t)


def _lane_prefix_sum(x):
    n = x.shape[-1]
    lane = lax.broadcasted_iota(jnp.int32, x.shape, x.ndim - 1)
    s = 1
    while s < n:
        x = x + jnp.where(lane >= s, pltpu.roll(x, s, x.ndim - 1), 0)
        s *= 2
    return x


def _route_kernel(cap, lg_ref, idx_ref, gate_ref, keyt_ref, key_ref, aff_ref):
    n = lg_ref.shape[1]
    b = pl.program_id(0)
    lg = lg_ref[...]
    m = jnp.max(lg, axis=0, keepdims=True)
    ex = jnp.exp(lg - m)
    aff = ex / jnp.sum(ex, axis=0, keepdims=True)
    aff_ref[...] = aff
    bits = pltpu.bitcast(aff, jnp.int32)

    def bis(_, carry):
        lo, hi = carry
        mid = lo + ((hi - lo + 1) >> 1)
        cnt = jnp.sum((bits >= mid).astype(jnp.int32), axis=1, keepdims=True)
        ok = cnt >= cap
        return jnp.where(ok, mid, lo), jnp.where(ok, hi, mid - 1)

    lo0 = jnp.zeros((N_EXPERTS, 1), jnp.int32)
    hi0 = jnp.full((N_EXPERTS, 1), 0x7F800000, jnp.int32)
    thr, _ = lax.fori_loop(0, 32, bis, (lo0, hi0))
    gt = bits > thr
    eq = bits == thr
    need = cap - jnp.sum(gt.astype(jnp.int32), axis=1, keepdims=True)
    eq_rank = _lane_prefix_sum(eq.astype(jnp.int32)) - eq.astype(jnp.int32)
    sel = gt | (eq & (eq_rank < need))
    seli = sel.astype(jnp.int32)
    key = jnp.where(sel, _lane_prefix_sum(seli) - 1, -1)
    key_ref[...] = key

    keyf = jnp.concatenate([key.astype(F32), jnp.full((LANES - N_EXPERTS, n), -1.0, F32)], axis=0)
    keyt_ref[...] = keyf.T

    tok = lax.broadcasted_iota(jnp.int32, (1, n), 1).astype(F32)
    rt = 64 if cap >= 64 else cap
    base = (b * n).astype(F32)
    for e in range(N_EXPERTS):
        krow = key_ref[e:e + 1, :]
        arow = aff_ref[e:e + 1, :]

        def tile(t, carry, krow=krow, arow=arow, e=e):
            r0 = pl.multiple_of(t * rt, rt)
            rr = r0 + lax.broadcasted_iota(jnp.int32, (rt, 1), 0)
            p = krow == rr
            ids = jnp.sum(jnp.where(p, tok, 0.0), axis=1, keepdims=True) + base
            gts = jnp.sum(jnp.where(p, arow, 0.0), axis=1, keepdims=True)
            idx_ref[e, pl.ds(r0, rt), :] = ids.astype(jnp.int32)
            gate_ref[e, pl.ds(r0, rt), :] = gts
            return carry
        lax.fori_loop(0, cap // rt, tile, 0)


def _route_call(logits_t, cap):
    b, _, n = logits_t.shape
    return pl.pallas_call(
        functools.partial(_route_kernel, cap),
        grid=(b,),
        in_specs=[pl.BlockSpec((None, N_EXPERTS, n), lambda i: (i, 0, 0))],
        out_specs=[pl.BlockSpec((None, N_EXPERTS, cap, 1), lambda i: (i, 0, 0, 0)),
                   pl.BlockSpec((None, N_EXPERTS, cap, 1), lambda i: (i, 0, 0, 0)),
                   pl.BlockSpec((None, n, LANES), lambda i: (i, 0, 0))],
        out_shape=[jax.ShapeDtypeStruct((b, N_EXPERTS, cap, 1), jnp.int32),
                   jax.ShapeDtypeStruct((b, N_EXPERTS, cap, 1), F32),
                   jax.ShapeDtypeStruct((b, n, LANES), F32)],
        scratch_shapes=[pltpu.VMEM((N_EXPERTS, n), jnp.int32),
                        pltpu.VMEM((N_EXPERTS, n), F32)],
        compiler_params=_cparams(("arbitrary",)),
        name="route",
    )(logits_t)


FF_TILE = 256


def _ffn_kernel(idx_ref, gate_ref, h_hbm, wg_ref, wu_ref, wd_ref, o_ref, xg32_ref, xg_ref, acc_ref, sem):
    e = pl.program_id(0)
    f = pl.program_id(1)
    m = xg_ref.shape[0]

    def row_copy(r):
        return pltpu.make_async_copy(h_hbm.at[pl.ds(idx_ref[e, r], 1), :], xg32_ref.at[pl.ds(r, 1), :], sem)

    @pl.when(f == 0)
    def _gather():
        def issue(r, carry):
            row_copy(r).start()
            return carry
        lax.fori_loop(0, m, issue, 0)

        pltpu.make_async_copy(h_hbm.at[pl.ds(0, m), :], xg32_ref, sem).wait()
        xg_ref[...] = xg32_ref[...].astype(BF16)
        acc_ref[...] = jnp.zeros_like(acc_ref)

    wg = wg_ref[...].astype(BF16)
    wu = wu_ref[...].astype(BF16)
    wd = wd_ref[...].astype(BF16)
    mt = min(512, m)
    for i in range(m // mt):
        rows = pl.ds(i * mt, mt)
        xg = xg_ref[rows, :]
        g = jnp.dot(xg, wg, preferred_element_type=F32)
        u = jnp.dot(xg, wu, preferred_element_type=F32)
        hid = (g * jax.nn.sigmoid(g) * u).astype(BF16)
        acc_ref[rows, :] += jnp.dot(hid, wd, preferred_element_type=F32)

    @pl.when(f == pl.num_programs(1) - 1)
    def _finish():
        o_ref[...] = (acc_ref[...] * gate_ref[...]).astype(o_ref.dtype)


def _ffn_call(idx, gate, table, w_gate, w_up, w_down):
    n_e, m = idx.shape
    d = table.shape[1]
    grid_spec = pltpu.PrefetchScalarGridSpec(
        num_scalar_prefetch=1,
        grid=(n_e, D_FF // FF_TILE),
        in_specs=[
            pl.BlockSpec((None, m, 1), lambda e, f, idx: (e, 0, 0)),
            pl.BlockSpec(memory_space=pl.ANY),
            pl.BlockSpec((None, d, FF_TILE), lambda e, f, idx: (e, 0, f)),
            pl.BlockSpec((None, d, FF_TILE), lambda e, f, idx: (e, 0, f)),
            pl.BlockSpec((None, FF_TILE, d), lambda e, f, idx: (e, f, 0)),
        ],
        out_specs=pl.BlockSpec((None, m, d), lambda e, f, idx: (e, 0, 0)),
        scratch_shapes=[pltpu.VMEM((m, d), F32), pltpu.VMEM((m, d), BF16), pltpu.VMEM((m, d), F32),
                        pltpu.SemaphoreType.DMA],
    )
    return pl.pallas_call(
        _ffn_kernel,
        grid_spec=grid_spec,
        out_shape=jax.ShapeDtypeStruct((n_e, m, d), BF16),
        compiler_params=_cparams(("arbitrary", "arbitrary")),
        name="ffn",
    )(idx, gate, table, w_gate, w_up, w_down)


def _combine_kernel(final, *refs):
    if final:
        x_ref, kt_ref, y_ref, g2_ref, fg_ref, o_ref = refs
    else:
        x_ref, kt_ref, y_ref, g2_ref, o_ref = refs
    cap = y_ref.shape[1]
    kt = kt_ref[...]
    lane = lax.broadcasted_iota(jnp.int32, (1, cap), 1).astype(F32)
    acc = jnp.zeros(x_ref.shape, F32)
    for e in range(N_EXPERTS):
        q = (kt[:, e:e + 1] == lane).astype(BF16)
        acc = acc + jnp.dot(q, y_ref[e], preferred_element_type=F32)
    x = x_ref[...] + g2_ref[...] * acc
    if final:
        x = x * lax.rsqrt(jnp.mean(x * x, axis=-1, keepdims=True) + EPS) * fg_ref[...]
    o_ref[...] = x


def _combine_call(x, keyt, yexp, g2, final_g):
    b, n, d = x.shape
    cap = yexp.shape[2]
    tc = min(512, n)
    final = final_g is not None
    row = pl.BlockSpec((None, tc, d), lambda i, j: (i, j, 0))
    in_specs = [row,
                pl.BlockSpec((None, tc, LANES), lambda i, j: (i, j, 0)),
                pl.BlockSpec((N_EXPERTS, None, cap, d), lambda i, j: (0, i, 0, 0)),
                pl.BlockSpec((None, 1, d), lambda i, j: (i, 0, 0))]
    args = [x, keyt, yexp, g2]
    if final:
        in_specs.append(pl.BlockSpec((1, d), lambda i, j: (0, 0)))
        args.append(final_g.reshape(1, d))
    return pl.pallas_call(
        functools.partial(_combine_kernel, final),
        grid=(b, n // tc),
        in_specs=in_specs,
        out_specs=row,
        out_shape=jax.ShapeDtypeStruct((b, n, d), F32),
        compiler_params=_cparams(("arbitrary", "arbitrary")),
        name="combine",
    )(*args)


def _grid_pos_embed(rows, d):
    rr, cc = np.meshgrid(np.arange(rows, dtype=np.float64), np.arange(GRID_W, dtype=np.float64), indexing="ij")
    quarter = d // 4
    omega = 1.0 / (POS_BASE ** (np.arange(quarter, dtype=np.float64) / quarter))

    def enc(p):
        ang = p.reshape(-1)[:, None] * omega
        return np.concatenate([np.sin(ang), np.cos(ang)], axis=-1)

    return np.concatenate([enc(rr), enc(cc)], axis=-1).astype(np.float32)


def _moe(x, h2, logits_t, g2, w_gate, w_up, w_down, final_g):
    b, n, d = x.shape
    cap = EC_CAPACITY * n // N_EXPERTS
    idx, gate, keyt = _route_call(logits_t, cap)
    idx = idx.reshape(b, N_EXPERTS, cap).transpose(1, 0, 2).reshape(N_EXPERTS, b * cap)
    gate = gate.reshape(b, N_EXPERTS, cap).transpose(1, 0, 2).reshape(N_EXPERTS, b * cap, 1)
    yexp = _ffn_call(idx, gate, h2.reshape(b * n, d), w_gate, w_up, w_down)
    return _combine_call(x, keyt, yexp.reshape(N_EXPERTS, b, cap, d), g2, final_g)


def kernel(x, c, ctx, c_ctx, w_mod, b_mod, norm1_g, norm2_g, w_in, conv_w, conv_b, lru_wa, lru_ba,
           lru_wi, lru_bi, lru_lambda, w_out, w_router, w_gate, w_up, w_down, final_g):
    bsz, n, d = x.shape
    depth = w_mod.shape[0]
    pos = jnp.asarray(_grid_pos_embed(n // GRID_W, d))

    crows = jnp.zeros((SUBLANES, d), F32).at[:bsz].set(c).at[bsz].set(c_ctx)
    mod = _mod_call(crows, w_mod, b_mod)

    for l in range(depth):
        last = l == depth - 1
        mx = mod[l, :bsz].reshape(bsz, 1, N_MOD, d)
        sh1, sc1, g1, sh2, sc2, g2 = [mx[:, :, i, :] for i in range(N_MOD)]
        mc = jnp.broadcast_to(mod[l, bsz].reshape(1, 1, N_MOD, d), (bsz, 1, N_MOD, d))
        csh1, csc1, cg1, csh2, csc2, cg2 = [mc[:, :, i, :] for i in range(N_MOD)]
        w_in_bf = w_in[l]
        w_out_bf = w_out[l]
        w_router_t = w_router[l].T
        lp = _lru_params(conv_w[l], conv_b[l], lru_wa[l], lru_ba[l], lru_wi[l], lru_bi[l], lru_lambda[l])

        cuf, cux, cug = _proj_call(ctx, None, norm1_g[l], csh1, csc1, w_in_bf)
        cyr, ctx_states = _lru_call(cux, cug, lp, None)
        if not last:
            cyf = _fourier_call(cuf)
            ctx, ch2, clg = _mixout_call(ctx, cyf, cyr, w_out_bf, cg1, norm2_g[l], csh2, csc2, w_router_t)
            ctx = _moe(ctx, ch2, clg, cg2, w_gate[l], w_up[l], w_down[l], None)

        if l == 0:
            uf, ux, ug, x = _proj_call(x, pos, norm1_g[l], sh1, sc1, w_in_bf)
        else:
            uf, ux, ug = _proj_call(x, None, norm1_g[l], sh1, sc1, w_in_bf)
        yf = _fourier_call(uf)
        yr, _ = _lru_call(ux, ug, lp, ctx_states)
        x, h2, lg = _mixout_call(x, yf, yr, w_out_bf, g1, norm2_g[l], sh2, sc2, w_router_t)
        x = _moe(x, h2, lg, g2, w_gate[l], w_up[l], w_down[l], final_g if last else None)
    return x
```

```python
import functools
import math

import numpy as np
import jax
import jax.numpy as jnp
from jax import lax
from jax.experimental import pallas as pl
from jax.experimental.pallas import tpu as pltpu

D_MODEL = 1024
GRID_W = 64
POS_BASE = 10000.0
D_FOURIER = 512
FOURIER_GROUPS = 4
D_GROUP = D_FOURIER // FOURIER_GROUPS
D_LRU = 512
LRU_HEADS = 8
LRU_HEAD_DIM = D_LRU // LRU_HEADS
CONV_W = 4
LRU_C = 8.0
D_IN = D_FOURIER + 2 * D_LRU
N_EXPERTS = 16
EC_CAPACITY = 2
D_FF = 2816
N_MOD = 6
EPS = 1e-6

LANES = 128
SUBLANES = 8
DFT_BLOCK = 256
TOKEN_ROWS = D_MODEL // LANES
VMEM_LIMIT = 56 * 1024 * 1024

F32 = jnp.float32
BF16 = jnp.bfloat16


def _cparams(sem, **kw):
    return pltpu.CompilerParams(dimension_semantics=sem, vmem_limit_bytes=VMEM_LIMIT, **kw)


def _mod_kernel(c_ref, w_ref, b_ref, o_ref):
    c = c_ref[...]
    cond = c * jax.nn.sigmoid(c)
    o_ref[...] = jnp.dot(cond, w_ref[...], preferred_element_type=F32,
                         precision=lax.Precision.HIGHEST) + b_ref[...]


def _mod_call(crows, w_mod, b_mod):
    depth, d, nm = w_mod.shape
    tn = 1536
    return pl.pallas_call(
        _mod_kernel,
        grid=(depth, nm // tn),
        in_specs=[
            pl.BlockSpec((SUBLANES, d), lambda l, j: (0, 0)),
            pl.BlockSpec((None, d, tn), lambda l, j: (l, 0, j)),
            pl.BlockSpec((None, 1, tn), lambda l, j: (l, 0, j)),
        ],
        out_specs=pl.BlockSpec((None, SUBLANES, tn), lambda l, j: (l, 0, j)),
        out_shape=jax.ShapeDtypeStruct((depth, SUBLANES, nm), F32),
        compiler_params=_cparams(("arbitrary", "arbitrary")),
        name="mod",
    )(crows, w_mod, b_mod.reshape(depth, 1, nm))


def _rms_mod(x, g, shift, scale):
    y = x * lax.rsqrt(jnp.mean(x * x, axis=-1, keepdims=True) + EPS) * g
    return y * (1.0 + scale) + shift


def _proj_kernel(add_pos, *refs):
    if add_pos:
        x_ref, pos_ref, g_ref, sh_ref, sc_ref, w_ref, uf_ref, ux_ref, ug_ref, xp_ref, wbf_ref = refs
        x = x_ref[...] + pos_ref[...]
        xp_ref[...] = x
    else:
        x_ref, g_ref, sh_ref, sc_ref, w_ref, uf_ref, ux_ref, ug_ref, wbf_ref = refs
        x = x_ref[...]

    @pl.when((pl.program_id(0) == 0) & (pl.program_id(1) == 0))
    def _cast_w():
        wbf_ref[...] = w_ref[...].astype(BF16)

    h = _rms_mod(x, g_ref[...], sh_ref[...], sc_ref[...])
    u = jnp.dot(h.astype(BF16), wbf_ref[...], preferred_element_type=F32)
    uf_ref[...] = u[:, :D_FOURIER]
    ux_ref[...] = u[:, D_FOURIER:D_FOURIER + D_LRU]
    ug_ref[...] = u[:, D_FOURIER + D_LRU:]


def _proj_call(x, pos, g, shift, scale, w_in_bf):
    b, n, d = x.shape
    tm = min(512, n)
    add_pos = pos is not None
    row = pl.BlockSpec((None, tm, d), lambda i, j: (i, j, 0))
    vec_b = pl.BlockSpec((None, 1, d), lambda i, j: (i, 0, 0))
    in_specs = [row]
    args = [x]
    if add_pos:
        in_specs.append(pl.BlockSpec((tm, d), lambda i, j: (j, 0)))
        args.append(pos)
    in_specs += [pl.BlockSpec((1, d), lambda i, j: (0, 0)), vec_b, vec_b,
                 pl.BlockSpec((d, D_IN), lambda i, j: (0, 0))]
    args += [g.reshape(1, d), shift, scale, w_in_bf]
    part = pl.BlockSpec((None, tm, D_FOURIER), lambda i, j: (i, j, 0))
    out_specs = [part, part, part]
    out_shape = [jax.ShapeDtypeStruct((b, n, D_FOURIER), F32)] * 3
    if add_pos:
        out_specs.append(row)
        out_shape.append(jax.ShapeDtypeStruct((b, n, d), F32))
    return pl.pallas_call(
        functools.partial(_proj_kernel, add_pos),
        grid=(b, n // tm),
        in_specs=in_specs,
        out_specs=out_specs,
        out_shape=out_shape,
        scratch_shapes=[pltpu.VMEM((d, D_IN), BF16)],
        compiler_params=_cparams(("arbitrary", "arbitrary")),
        name="proj",
    )(*args)


def _cmul_const(re, im, c, s):
    tol = 1e-12
    if abs(s) < tol:
        if abs(c - 1.0) < tol:
            return re, im
        if abs(c + 1.0) < tol:
            return -re, -im
        return re * c, im * c
    if abs(c) < tol:
        if abs(s - 1.0) < tol:
            return -im, re
        if abs(s + 1.0) < tol:
            return im, -re
        return -im * s, re * s
    return re * c - im * s, re * s + im * c


def _fft_list(xs):
    n = len(xs)
    if n == 1:
        return xs
    ev = _fft_list(xs[0::2])
    od = _fft_list(xs[1::2])
    out = [None] * n
    for k in range(n // 2):
        ang = -2.0 * math.pi * k / n
        tr, ti = _cmul_const(od[k][0], od[k][1], math.cos(ang), math.sin(ang))
        out[k] = (ev[k][0] + tr, ev[k][1] + ti)
        out[k + n // 2] = (ev[k][0] - tr, ev[k][1] - ti)
    return out


def _fourier_kernel(n1_count, u_ref, cs_ref, twc_ref, tws_ref, cc_ref, o_ref, tr_ref, ti_ref, pq_ref):
    n = u_ref.shape[0]
    cs = cs_ref[...].astype(BF16)
    for n1 in range(n1_count):
        if n1_count == 1:
            z = u_ref[...]
        else:
            z = u_ref[pl.ds(n1, DFT_BLOCK, stride=n1_count), :]
        g = jnp.dot(cs, z.astype(BF16), preferred_element_type=F32)
        gr = g[:DFT_BLOCK]
        gs = g[DFT_BLOCK:]
        if n1 == 0:
            tr_ref[n1] = gr
            ti_ref[n1] = -gs
        else:
            c = twc_ref[n1]
            s = tws_ref[n1]
            tr_ref[n1] = gr * c - gs * s
            ti_ref[n1] = -(gr * s + gs * c)

    if n1_count == 1:
        pq_ref[:, :D_GROUP] = tr_ref[0]
        pq_ref[:, D_GROUP:] = -ti_ref[0]
    else:
        def chunk(j, carry):
            r0 = pl.multiple_of(j * SUBLANES, SUBLANES)
            xs = [(tr_ref[i, pl.ds(r0, SUBLANES), :], ti_ref[i, pl.ds(r0, SUBLANES), :])
                  for i in range(n1_count)]
            ys = _fft_list(xs)
            for k1 in range(n1_count):
                rows = pl.ds(pl.multiple_of(k1 * DFT_BLOCK + r0, SUBLANES), SUBLANES)
                pq_ref[rows, :D_GROUP] = ys[k1][0]
                pq_ref[rows, D_GROUP:] = -ys[k1][1]
            return carry
        lax.fori_loop(0, DFT_BLOCK // SUBLANES, chunk, 0)

    cc = cc_ref[...].astype(BF16)
    tile = min(512, n)
    for i in range(n // tile):
        rows = pl.ds(i * tile, tile)
        o_ref[rows, :] = jnp.dot(pq_ref[rows, :].astype(BF16), cc, preferred_element_type=F32)


@functools.lru_cache(maxsize=None)
def _fourier_consts(n):
    n1c = n // DFT_BLOCK
    k = np.arange(DFT_BLOCK, dtype=np.float64)
    ang = 2.0 * np.pi * np.outer(k, k) / DFT_BLOCK
    cs = np.concatenate([np.cos(ang), np.sin(ang)], axis=0)
    tw = 2.0 * np.pi * np.outer(np.arange(n1c, dtype=np.float64), k) / n
    twc = np.repeat(np.cos(tw)[:, :, None], LANES, axis=2)
    tws = np.repeat(np.sin(tw)[:, :, None], LANES, axis=2)
    c = np.arange(D_GROUP, dtype=np.float64)
    angc = 2.0 * np.pi * np.outer(c, c) / D_GROUP
    scale = 1.0 / math.sqrt(n * D_GROUP)
    cc = np.concatenate([np.cos(angc), -np.sin(angc)], axis=0) * scale
    return (cs.astype(np.float32), twc.astype(np.float32), tws.astype(np.float32), cc.astype(np.float32))


def _fourier_call(u):
    b, n, _ = u.shape
    n1c = n // DFT_BLOCK
    cs, twc, tws, cc = _fourier_consts(n)
    blk = pl.BlockSpec((None, n, D_GROUP), lambda i, j: (i, 0, j))
    return pl.pallas_call(
        functools.partial(_fourier_kernel, n1c),
        grid=(b, FOURIER_GROUPS),
        in_specs=[
            blk,
            pl.BlockSpec((2 * DFT_BLOCK, DFT_BLOCK), lambda i, j: (0, 0)),
            pl.BlockSpec((n1c, DFT_BLOCK, LANES), lambda i, j: (0, 0, 0)),
            pl.BlockSpec((n1c, DFT_BLOCK, LANES), lambda i, j: (0, 0, 0)),
            pl.BlockSpec((2 * D_GROUP, D_GROUP), lambda i, j: (0, 0)),
        ],
        out_specs=blk,
        out_shape=jax.ShapeDtypeStruct(u.shape, F32),
        scratch_shapes=[
            pltpu.VMEM((n1c, DFT_BLOCK, D_GROUP), F32),
            pltpu.VMEM((n1c, DFT_BLOCK, D_GROUP), F32),
            pltpu.VMEM((n, 2 * D_GROUP), F32),
        ],
        compiler_params=_cparams(("arbitrary", "arbitrary")),
        name="fourier",
    )(u, jnp.asarray(cs), jnp.asarray(twc), jnp.asarray(tws), jnp.asarray(cc))


SEGMENTS = SUBLANES
PAD_ROWS = 8


def _expm1(x):
    u = jnp.exp(x)
    near = (u - 1.0) * x / jnp.log(jnp.where(u == 1.0, 2.0, u))
    return jnp.where(x < -0.5, u - 1.0, jnp.where(u == 1.0, x, near))


def _lru_kernel(has_h0, *refs):
    if has_h0:
        (ux_ref, ug_ref, cw_ref, cb_ref, wg_ref, bg_ref, lam_ref, h0_ref,
         y_ref, fin_ref, upad_ref, a_ref, b_ref, h_ref) = refs
    else:
        (ux_ref, ug_ref, cw_ref, cb_ref, wg_ref, bg_ref, lam_ref,
         y_ref, fin_ref, upad_ref, a_ref, b_ref, h_ref) = refs
    n = ux_ref.shape[0]
    seg_len = n // SEGMENTS
    tile = min(256, seg_len)

    zeros_pad = jnp.zeros((PAD_ROWS, LANES), F32)
    upad_ref[pl.ds(0, PAD_ROWS), :] = zeros_pad
    upad_ref[pl.ds(PAD_ROWS + n, PAD_ROWS), :] = zeros_pad
    upad_ref[pl.ds(PAD_ROWS, n), :] = ux_ref[...]

    lam = lam_ref[...]
    sp = jnp.maximum(-lam, 0.0) + jnp.log1p(jnp.exp(-jnp.abs(lam)))
    cw = cw_ref[...]
    cb = cb_ref[...]
    wg = wg_ref[...].astype(BF16)
    bg = bg_ref[...]
    left = (CONV_W - 1) // 2

    for t in range(n // tile):
        r0 = t * tile
        xc = cb
        for k in range(CONV_W):
            xc = xc + upad_ref[pl.ds(PAD_ROWS + r0 + k - left, tile), :] * cw[k:k + 1, :]
        gates = jnp.dot(xc.astype(BF16), wg, preferred_element_type=F32) + bg
        seg = r0 // seg_len
        s0 = r0 - seg * seg_len
        for d in range(2):
            r = jax.nn.sigmoid(gates[:, (2 * d) * LANES:(2 * d + 1) * LANES])
            gi = jax.nn.sigmoid(gates[:, (2 * d + 1) * LANES:(2 * d + 2) * LANES])
            log_a = (-LRU_C) * r * sp[d:d + 1, :]
            a = jnp.exp(log_a)
            drive = jnp.sqrt(-_expm1(2.0 * log_a)) * (gi * xc)
            dst = pl.ds(s0 * SEGMENTS + seg, tile, stride=SEGMENTS)
            a_ref[d, dst, :] = a
            b_ref[d, dst, :] = drive

    def rows(s):
        return pl.ds(pl.multiple_of(s * SEGMENTS, SEGMENTS), SEGMENTS)

    def pass1(t, carry):
        hf, af, hb, ab = carry
        sb = seg_len - 1 - t
        a0 = a_ref[0, rows(t), :]
        a1 = a_ref[1, rows(sb), :]
        hf = a0 * hf + b_ref[0, rows(t), :]
        hb = a1 * hb + b_ref[1, rows(sb), :]
        return hf, af * a0, hb, ab * a1

    z = jnp.zeros((SEGMENTS, LANES), F32)
    o = jnp.ones((SEGMENTS, LANES), F32)
    hf, af, hb, ab = lax.fori_loop(0, seg_len, pass1, (z, o, z, o), unroll=8)

    if has_h0:
        cf = h0_ref[0:1, :]
        cbk = h0_ref[1:2, :]
    else:
        cf = jnp.zeros((1, LANES), F32)
        cbk = jnp.zeros((1, LANES), F32)
    sub = lax.broadcasted_iota(jnp.int32, (SEGMENTS, LANES), 0)
    carry_f = z
    for j in range(SEGMENTS):
        carry_f = jnp.where(sub == j, cf, carry_f)
        cf = hf[j:j + 1, :] + af[j:j + 1, :] * cf
    carry_b = z
    for j in reversed(range(SEGMENTS)):
        carry_b = jnp.where(sub == j, cbk, carry_b)
        cbk = hb[j:j + 1, :] + ab[j:j + 1, :] * cbk
    fin_ref[0:1, :] = cf
    fin_ref[1:2, :] = cbk

    def pass2(t, carry):
        hf, hb = carry
        sb = seg_len - 1 - t
        hf = a_ref[0, rows(t), :] * hf + b_ref[0, rows(t), :]
        h_ref[0, rows(t), :] = hf
        hb = a_ref[1, rows(sb), :] * hb + b_ref[1, rows(sb), :]
        h_ref[1, rows(sb), :] = hb
        return hf, hb

    lax.fori_loop(0, seg_len, pass2, (carry_f, carry_b), unroll=8)

    for t in range(n // tile):
        r0 = t * tile
        seg = r0 // seg_len
        s0 = r0 - seg * seg_len
        src = pl.ds(s0 * SEGMENTS + seg, tile, stride=SEGMENTS)
        hsum = h_ref[0, src, :] + h_ref[1, src, :]
        y_ref[pl.ds(r0, tile), :] = hsum * jax.nn.gelu(ug_ref[pl.ds(r0, tile), :])


def _lru_call(ux, ug, lp, h0):
    b, n, _ = ux.shape
    nblk = D_LRU // LANES
    has_h0 = h0 is not None
    blk = pl.BlockSpec((None, n, LANES), lambda i, j: (i, 0, j))
    in_specs = [
        blk, blk,
        pl.BlockSpec((CONV_W, LANES), lambda i, j: (0, j)),
        pl.BlockSpec((1, LANES), lambda i, j: (0, j)),
        pl.BlockSpec((None, LANES, 4 * LANES), lambda i, j: (j, 0, 0)),
        pl.BlockSpec((None, 1, 4 * LANES), lambda i, j: (j, 0, 0)),
        pl.BlockSpec((2, LANES), lambda i, j: (0, j)),
    ]
    args = [ux, ug, lp["conv_w"], lp["conv_b"], lp["w_gates"], lp["b_gates"], lp["lam"]]
    if has_h0:
        in_specs.append(pl.BlockSpec((None, 2, LANES), lambda i, j: (i, 0, j)))
        args.append(h0)
    return pl.pallas_call(
        functools.partial(_lru_kernel, has_h0),
        grid=(b, nblk),
        in_specs=in_specs,
        out_specs=[blk, pl.BlockSpec((None, 2, LANES), lambda i, j: (i, 0, j))],
        out_shape=[jax.ShapeDtypeStruct((b, n, D_LRU), F32),
                   jax.ShapeDtypeStruct((b, 2, D_LRU), F32)],
        scratch_shapes=[
            pltpu.VMEM((n + 2 * PAD_ROWS, LANES), F32),
            pltpu.VMEM((2, n, LANES), F32),
            pltpu.VMEM((2, n, LANES), F32),
            pltpu.VMEM((2, n, LANES), F32),
        ],
        compiler_params=_cparams(("arbitrary", "arbitrary")),
        name="lru",
    )(*args)


def _lru_params(conv_w, conv_b, wa, ba, wi, bi, lam):
    nblk = D_LRU // LANES
    hpb = LANES // LRU_HEAD_DIM

    def blockdiag(w):
        w = w.reshape(nblk, hpb, LRU_HEAD_DIM, LRU_HEAD_DIM)
        eye = jnp.eye(hpb, dtype=w.dtype)
        return jnp.einsum("bhij,hg->bhigj", w, eye).reshape(nblk, LANES, LANES)

    w_gates = jnp.concatenate([blockdiag(wa[0]), blockdiag(wi[0]), blockdiag(wa[1]), blockdiag(wi[1])], axis=-1)
    b_gates = jnp.stack([ba[0], bi[0], ba[1], bi[1]], axis=0)
    b_gates = b_gates.reshape(4, nblk, LANES).transpose(1, 0, 2).reshape(nblk, 1, 4 * LANES)
    return {"conv_w": conv_w, "conv_b": conv_b.reshape(1, D_LRU), "w_gates": w_gates,
            "b_gates": b_gates, "lam": lam}


def _mixout_kernel(x_ref, yf_ref, yr_ref, w_ref, g1_ref, n2_ref, sh_ref, sc_ref, wr_ref,
                   xo_ref, h2_ref, lg_ref, wbf_ref):
    @pl.when((pl.program_id(0) == 0) & (pl.program_id(1) == 0))
    def _cast_w():
        wbf_ref[...] = w_ref[...].astype(BF16)

    mix = jnp.dot(yf_ref[...].astype(BF16), wbf_ref[:D_FOURIER, :], preferred_element_type=F32)
    mix = mix + jnp.dot(yr_ref[...].astype(BF16), wbf_ref[D_FOURIER:, :], preferred_element_type=F32)
    x = x_ref[...] + g1_ref[...] * mix
    xo_ref[...] = x
    h2 = _rms_mod(x, n2_ref[...], sh_ref[...], sc_ref[...])
    tm = h2.shape[0]
    for s in range(TOKEN_ROWS):
        h2_ref[pl.ds(s, tm, stride=TOKEN_ROWS), :] = h2[:, s * LANES:(s + 1) * LANES]
    lg_ref[...] = lax.dot_general(wr_ref[...], h2, (((1,), (1,)), ((), ())),
                                  preferred_element_type=F32, precision=lax.Precision.HIGHEST)


def _mixout_call(x, yf, yr, w_out_bf, g1, n2g, sh2, sc2, w_router_t):
    b, n, d = x.shape
    tm = min(512, n)
    row = pl.BlockSpec((None, tm, d), lambda i, j: (i, j, 0))
    half = pl.BlockSpec((None, tm, D_FOURIER), lambda i, j: (i, j, 0))
    vec_b = pl.BlockSpec((None, 1, d), lambda i, j: (i, 0, 0))
    return pl.pallas_call(
        _mixout_kernel,
        grid=(b, n // tm),
        in_specs=[row, half, half,
                  pl.BlockSpec((d, d), lambda i, j: (0, 0)),
                  vec_b,
                  pl.BlockSpec((1, d), lambda i, j: (0, 0)),
                  vec_b, vec_b,
                  pl.BlockSpec((N_EXPERTS, d), lambda i, j: (0, 0))],
        out_specs=[row,
                   pl.BlockSpec((None, tm * TOKEN_ROWS, LANES), lambda i, j: (i, j, 0)),
                   pl.BlockSpec((None, N_EXPERTS, tm), lambda i, j: (i, 0, j))],
        out_shape=[jax.ShapeDtypeStruct((b, n, d), F32),
                   jax.ShapeDtypeStruct((b, n * TOKEN_ROWS, LANES), F32),
                   jax.ShapeDtypeStruct((b, N_EXPERTS, n), F32)],
        scratch_shapes=[pltpu.VMEM((d, d), BF16)],
        compiler_params=_cparams(("arbitrary", "arbitrary")),
        name="mixout",
    )(x, yf, yr, w_out_bf, g1, n2g.reshape(1, d), sh2, sc2, w_router_t)


def _lane_prefix_sum(x):
    n = x.shape[-1]
    lane = lax.broadcasted_iota(jnp.int32, x.shape, x.ndim - 1)
    s = 1
    while s < n:
        x = x + jnp.where(lane >= s, pltpu.roll(x, s, x.ndim - 1), 0)
        s *= 2
    return x


def _route_kernel(cap, lg_ref, idx_ref, gate_ref, keyt_ref, key_ref, aff_ref):
    n = lg_ref.shape[1]
    b = pl.program_id(0)
    lg = lg_ref[...]
    m = jnp.max(lg, axis=0, keepdims=True)
    ex = jnp.exp(lg - m)
    aff = ex / jnp.sum(ex, axis=0, keepdims=True)
    aff_ref[...] = aff
    bits = pltpu.bitcast(aff, jnp.int32)

    def bis(_, carry):
        lo, hi = carry
        mid = lo + ((hi - lo + 1) >> 1)
        cnt = jnp.sum((bits >= mid).astype(jnp.int32), axis=1, keepdims=True)
        ok = cnt >= cap
        return jnp.where(ok, mid, lo), jnp.where(ok, hi, mid - 1)

    lo0 = jnp.zeros((N_EXPERTS, 1), jnp.int32)
    hi0 = jnp.full((N_EXPERTS, 1), 0x7F800000, jnp.int32)
    thr, _ = lax.fori_loop(0, 32, bis, (lo0, hi0))

    thr_f = pltpu.bitcast(thr, F32)
    window = 2.0 ** -6
    flo0 = thr_f * (1.0 - window)
    fhi0 = jnp.maximum(thr_f * (1.0 + window), jnp.float32(1e-37))

    def fbis(_, carry):
        lo, hi = carry
        mid = lo + 0.5 * (hi - lo)
        cnt = jnp.sum((aff >= mid).astype(jnp.int32), axis=1, keepdims=True)
        ok = cnt >= cap
        return jnp.where(ok, mid, lo), jnp.where(ok, hi, mid)

    flo, fhi = lax.fori_loop(0, 32, fbis, (flo0, fhi0))
    gt = aff >= fhi
    eq = (aff >= flo) & (aff < fhi)
    need = cap - jnp.sum(gt.astype(jnp.int32), axis=1, keepdims=True)
    eq_rank = _lane_prefix_sum(eq.astype(jnp.int32)) - eq.astype(jnp.int32)
    sel = gt | (eq & (eq_rank < need))
    seli = sel.astype(jnp.int32)
    key = jnp.where(sel, _lane_prefix_sum(seli) - 1, -1)
    key_ref[...] = key

    keyf = jnp.concatenate([key.astype(F32), jnp.full((LANES - N_EXPERTS, n), -1.0, F32)], axis=0)
    keyt_ref[...] = keyf.T

    tok = lax.broadcasted_iota(jnp.int32, (1, n), 1).astype(F32)
    rt = 64 if cap >= 64 else cap
    base = (b * n).astype(F32)
    for e in range(N_EXPERTS):
        krow = key_ref[e:e + 1, :]
        arow = aff_ref[e:e + 1, :]

        def tile(t, carry, krow=krow, arow=arow, e=e):
            r0 = pl.multiple_of(t * rt, rt)
            rr = r0 + lax.broadcasted_iota(jnp.int32, (rt, 1), 0)
            p = krow == rr
            ids = jnp.sum(jnp.where(p, tok, 0.0), axis=1, keepdims=True) + base
            gts = jnp.sum(jnp.where(p, arow, 0.0), axis=1, keepdims=True)
            idx_ref[e, pl.ds(r0, rt), :] = ids.astype(jnp.int32)
            gate_ref[e, pl.ds(r0, rt), :] = gts
            return carry
        lax.fori_loop(0, cap // rt, tile, 0)


def _route_call(logits_t, cap):
    b, _, n = logits_t.shape
    return pl.pallas_call(
        functools.partial(_route_kernel, cap),
        grid=(b,),
        in_specs=[pl.BlockSpec((None, N_EXPERTS, n), lambda i: (i, 0, 0))],
        out_specs=[pl.BlockSpec((None, N_EXPERTS, cap, 1), lambda i: (i, 0, 0, 0)),
                   pl.BlockSpec((None, N_EXPERTS, cap, 1), lambda i: (i, 0, 0, 0)),
                   pl.BlockSpec((None, n, LANES), lambda i: (i, 0, 0))],
        out_shape=[jax.ShapeDtypeStruct((b, N_EXPERTS, cap, 1), jnp.int32),
                   jax.ShapeDtypeStruct((b, N_EXPERTS, cap, 1), F32),
                   jax.ShapeDtypeStruct((b, n, LANES), F32)],
        scratch_shapes=[pltpu.VMEM((N_EXPERTS, n), jnp.int32),
                        pltpu.VMEM((N_EXPERTS, n), F32)],
        compiler_params=_cparams(("arbitrary",)),
        name="route",
    )(logits_t)


FF_TILE = 256


FF_STEPS = D_FF // FF_TILE
ROW_ALIGN = 32


def _round_up(v, m):
    return -(-v // m) * m


def _ffn_layout(row_counts):
    layout, off = [], 0
    for m in row_counts:
        padded = _round_up(m, FF_STEPS * SUBLANES)
        layout.append((m, padded, off))
        off = _round_up(off + padded, ROW_ALIGN)
    out_rows = _round_up(layout[-1][2] + layout[-1][0], ROW_ALIGN)
    scratch_rows = _round_up(layout[-1][2] + layout[-1][1], SUBLANES)
    return tuple(layout), out_rows, max(scratch_rows, out_rows)


def _ffn_kernel(layout, *refs):
    ns = len(layout)
    idx_refs = refs[:ns]
    gate_ref = refs[ns]
    tabs = refs[ns + 1:2 * ns + 1]
    wg_ref, wu_ref, wd_ref, o_ref, xg32_ref, xg_ref, acc_ref, sem = refs[2 * ns + 1:]
    e = pl.program_id(0)
    f = pl.program_id(1)
    n_e = pl.num_programs(0)
    slot = e % 2

    def issue_chunk(expert, dst_slot, step):
        for s, (_, padded, off) in enumerate(layout):
            chunk = padded // FF_STEPS
            src0 = expert * padded + step * chunk
            dst0 = off + step * chunk

            def body(g, carry, s=s, src0=src0, dst0=dst0):
                src = src0 + g * SUBLANES
                dst = pl.multiple_of(dst0 + g * SUBLANES, SUBLANES)
                for j in range(SUBLANES):
                    tok = idx_refs[s][src + j]
                    pltpu.make_async_copy(
                        tabs[s].at[pl.ds(pl.multiple_of(tok * TOKEN_ROWS, TOKEN_ROWS), TOKEN_ROWS), :],
                        xg32_ref.at[dst_slot, pl.ds(pl.multiple_of((dst + j) * TOKEN_ROWS, TOKEN_ROWS), TOKEN_ROWS), :],
                        sem.at[dst_slot]).start()
                return carry
            lax.fori_loop(0, chunk // SUBLANES, body, 0)

    @pl.when((e == 0) & (f == 0))
    def _first_expert():
        for step in range(FF_STEPS):
            issue_chunk(0, 0, step)

    @pl.when(f == 0)
    def _start_expert():
        for s, (m, padded, off) in enumerate(layout):
            pltpu.make_async_copy(tabs[s].at[pl.ds(0, padded * TOKEN_ROWS), :],
                                  xg32_ref.at[slot, pl.ds(off * TOKEN_ROWS, padded * TOKEN_ROWS), :],
                                  sem.at[slot]).wait()
        pack = 2 * SUBLANES
        for m, _, off in layout:
            def unpack(g, carry, off=off):
                t0 = pl.multiple_of(off + g * pack, pack)
                for c in range(TOKEN_ROWS):
                    v = xg32_ref[slot, pl.ds(t0 * TOKEN_ROWS + c, pack, stride=TOKEN_ROWS), :]
                    xg_ref[pl.ds(t0, pack), c * LANES:(c + 1) * LANES] = v.astype(BF16)
                return carry
            lax.fori_loop(0, m // pack, unpack, 0)
            acc_ref[pl.ds(off, m), :] = jnp.zeros((m, acc_ref.shape[1]), F32)

    @pl.when(e < n_e - 1)
    def _prefetch():
        issue_chunk(e + 1, 1 - slot, f)

    wg = wg_ref[...].astype(BF16)
    wu = wu_ref[...].astype(BF16)
    wd = wd_ref[...].astype(BF16)
    for m, _, off in layout:
        mt = min(512, m)
        for i in range(m // mt):
            rows = pl.ds(off + i * mt, mt)
            xg = xg_ref[rows, :]
            g = jnp.dot(xg, wg, preferred_element_type=F32)
            u = jnp.dot(xg, wu, preferred_element_type=F32)
            hid = (g * jax.nn.sigmoid(g) * u).astype(BF16)
            acc_ref[rows, :] += jnp.dot(hid, wd, preferred_element_type=F32)

    @pl.when(f == FF_STEPS - 1)
    def _finish():
        end = 0
        for m, _, off in layout:
            if off > end:
                o_ref[pl.ds(end, off - end), :] = jnp.zeros((off - end, o_ref.shape[1]), o_ref.dtype)
            o_ref[pl.ds(off, m), :] = (acc_ref[pl.ds(off, m), :] * gate_ref[pl.ds(off, m), :]).astype(o_ref.dtype)
            end = off + m
        if o_ref.shape[0] > end:
            o_ref[pl.ds(end, o_ref.shape[0] - end), :] = jnp.zeros((o_ref.shape[0] - end, o_ref.shape[1]), o_ref.dtype)


def _ffn_call(layer, row_counts, idxs, gate, tables, w_gate, w_up, w_down):
    ns = len(tables)
    n_e = w_gate.shape[1]
    d = w_gate.shape[2]
    layout, out_rows, scratch_rows = _ffn_layout(row_counts)
    assert gate.shape == (n_e, out_rows, 1)
    grid_spec = pltpu.PrefetchScalarGridSpec(
        num_scalar_prefetch=ns,
        grid=(n_e, FF_STEPS),
        in_specs=[pl.BlockSpec((None, out_rows, 1), lambda e, f, *_: (e, 0, 0))]
        + [pl.BlockSpec(memory_space=pl.ANY)] * ns
        + [pl.BlockSpec((None, None, d, FF_TILE), lambda e, f, *_: (layer, e, 0, f)),
           pl.BlockSpec((None, None, d, FF_TILE), lambda e, f, *_: (layer, e, 0, f)),
           pl.BlockSpec((None, None, FF_TILE, d), lambda e, f, *_: (layer, e, f, 0))],
        out_specs=pl.BlockSpec((None, out_rows, d), lambda e, f, *_: (e, 0, 0)),
        scratch_shapes=[pltpu.VMEM((2, scratch_rows * TOKEN_ROWS, LANES), F32), pltpu.VMEM((scratch_rows, d), BF16),
                        pltpu.VMEM((scratch_rows, d), F32), pltpu.SemaphoreType.DMA((2,))],
    )
    return pl.pallas_call(
        functools.partial(_ffn_kernel, layout),
        grid_spec=grid_spec,
        out_shape=jax.ShapeDtypeStruct((n_e, out_rows, d), BF16),
        compiler_params=_cparams(("arbitrary", "arbitrary")),
        name="ffn",
    )(*idxs, gate, *tables, w_gate, w_up, w_down)


def _combine_kernel(final, *refs):
    if final:
        x_ref, kt_ref, y_ref, g2_ref, fg_ref, o_ref = refs
    else:
        x_ref, kt_ref, y_ref, g2_ref, o_ref = refs
    cap = y_ref.shape[1]
    kt = kt_ref[...]
    lane = lax.broadcasted_iota(jnp.int32, (1, cap), 1).astype(F32)
    acc = jnp.zeros(x_ref.shape, F32)
    for e in range(N_EXPERTS):
        q = (kt[:, e:e + 1] == lane).astype(BF16)
        acc = acc + jnp.dot(q, y_ref[e], preferred_element_type=F32)
    x = x_ref[...] + g2_ref[...] * acc
    if final:
        x = x * lax.rsqrt(jnp.mean(x * x, axis=-1, keepdims=True) + EPS) * fg_ref[...]
    o_ref[...] = x


def _combine_call(x, keyt, yexp, first_row, cap, g2, final_g):
    b, n, d = x.shape
    tc = min(512, n)
    final = final_g is not None
    blk0 = first_row // cap
    row = pl.BlockSpec((None, tc, d), lambda i, j: (i, j, 0))
    in_specs = [row,
                pl.BlockSpec((None, tc, LANES), lambda i, j: (i, j, 0)),
                pl.BlockSpec((N_EXPERTS, cap, d), lambda i, j: (0, blk0 + i, 0)),
                pl.BlockSpec((None, 1, d), lambda i, j: (i, 0, 0))]
    args = [x, keyt, yexp, g2]
    if final:
        in_specs.append(pl.BlockSpec((1, d), lambda i, j: (0, 0)))
        args.append(final_g.reshape(1, d))
    return pl.pallas_call(
        functools.partial(_combine_kernel, final),
        grid=(b, n // tc),
        in_specs=in_specs,
        out_specs=row,
        out_shape=jax.ShapeDtypeStruct((b, n, d), F32),
        compiler_params=_cparams(("arbitrary", "arbitrary")),
        name="combine",
    )(*args)


def _grid_pos_embed(rows, d):
    rr, cc = np.meshgrid(np.arange(rows, dtype=np.float64), np.arange(GRID_W, dtype=np.float64), indexing="ij")
    quarter = d // 4
    omega = 1.0 / (POS_BASE ** (np.arange(quarter, dtype=np.float64) / quarter))

    def enc(p):
        ang = p.reshape(-1)[:, None] * omega
        return np.concatenate([np.sin(ang), np.cos(ang)], axis=-1)

    return np.concatenate([enc(rr), enc(cc)], axis=-1).astype(np.float32)


def _moe(layer, streams, w_gate, w_up, w_down):
    d = streams[0]["x"].shape[-1]
    caps = [EC_CAPACITY * s["x"].shape[1] // N_EXPERTS for s in streams]
    row_counts = [s["x"].shape[0] * cap for s, cap in zip(streams, caps)]
    layout, out_rows, _ = _ffn_layout(row_counts)
    idxs, keyts = [], []
    gate_all = jnp.zeros((N_EXPERTS, out_rows, 1), F32)
    for s, cap, (m, padded, off) in zip(streams, caps, layout):
        b = s["x"].shape[0]
        idx, gate, keyt = _route_call(s["logits_t"], cap)
        idx = idx.reshape(b, N_EXPERTS, cap).transpose(1, 0, 2).reshape(N_EXPERTS, m)
        idxs.append(jnp.pad(idx, ((0, 0), (0, padded - m))).reshape(N_EXPERTS * padded))
        gate = gate.reshape(b, N_EXPERTS, cap).transpose(1, 0, 2).reshape(N_EXPERTS, m, 1)
        gate_all = lax.dynamic_update_slice(gate_all, gate, (0, off, 0))
        keyts.append(keyt)
    tables = [s["h2"].reshape(-1, LANES) for s in streams]
    yexp = _ffn_call(layer, row_counts, idxs, gate_all, tables, w_gate, w_up, w_down)
    return [_combine_call(s["x"], keyt, yexp, off, cap, s["g2"], s["final_g"])
            for s, keyt, cap, (_, _, off) in zip(streams, keyts, caps, layout)]


def kernel(x, c, ctx, c_ctx, w_mod, b_mod, norm1_g, norm2_g, w_in, conv_w, conv_b, lru_wa, lru_ba,
           lru_wi, lru_bi, lru_lambda, w_out, w_router, w_gate, w_up, w_down, final_g):
    bsz, n, d = x.shape
    depth = w_mod.shape[0]
    pos = jnp.asarray(_grid_pos_embed(n // GRID_W, d))

    crows = jnp.zeros((SUBLANES, d), F32).at[:bsz].set(c).at[bsz].set(c_ctx)
    mod = _mod_call(crows, w_mod, b_mod)

    for l in range(depth):
        last = l == depth - 1
        mx = mod[l, :bsz].reshape(bsz, 1, N_MOD, d)
        sh1, sc1, g1, sh2, sc2, g2 = [mx[:, :, i, :] for i in range(N_MOD)]
        mc = jnp.broadcast_to(mod[l, bsz].reshape(1, 1, N_MOD, d), (bsz, 1, N_MOD, d))
        csh1, csc1, cg1, csh2, csc2, cg2 = [mc[:, :, i, :] for i in range(N_MOD)]
        w_in_bf = w_in[l]
        w_out_bf = w_out[l]
        w_router_t = w_router[l].T
        lp = _lru_params(conv_w[l], conv_b[l], lru_wa[l], lru_ba[l], lru_wi[l], lru_bi[l], lru_lambda[l])

        cuf, cux, cug = _proj_call(ctx, None, norm1_g[l], csh1, csc1, w_in_bf)
        cyr, ctx_states = _lru_call(cux, cug, lp, None)
        streams = []
        if not last:
            cyf = _fourier_call(cuf)
            ctx, ch2, clg = _mixout_call(ctx, cyf, cyr, w_out_bf, cg1, norm2_g[l], csh2, csc2, w_router_t)
            streams.append(dict(x=ctx, h2=ch2, logits_t=clg, g2=cg2, final_g=None))

        if l == 0:
            uf, ux, ug, x = _proj_call(x, pos, norm1_g[l], sh1, sc1, w_in_bf)
        else:
            uf, ux, ug = _proj_call(x, None, norm1_g[l], sh1, sc1, w_in_bf)
        yf = _fourier_call(uf)
        yr, _ = _lru_call(ux, ug, lp, ctx_states)
        x, h2, lg = _mixout_call(x, yf, yr, w_out_bf, g1, norm2_g[l], sh2, sc2, w_router_t)
        streams.insert(0, dict(x=x, h2=h2, logits_t=lg, g2=g2, final_g=final_g if last else None))
        outs = _moe(l, streams, w_gate, w_up, w_down)
        x = outs[0]
        if not last:
            ctx = outs[1]
    return x
```

```python
import functools
import math

import numpy as np
import jax
import jax.numpy as jnp
from jax import lax
from jax.experimental import pallas as pl
from jax.experimental.pallas import tpu as pltpu

D_MODEL = 1024
GRID_W = 64
POS_BASE = 10000.0
D_FOURIER = 512
FOURIER_GROUPS = 4
D_GROUP = D_FOURIER // FOURIER_GROUPS
D_LRU = 512
LRU_HEADS = 8
LRU_HEAD_DIM = D_LRU // LRU_HEADS
CONV_W = 4
LRU_C = 8.0
D_IN = D_FOURIER + 2 * D_LRU
N_EXPERTS = 16
EC_CAPACITY = 2
D_FF = 2816
N_MOD = 6
EPS = 1e-6

LANES = 128
SUBLANES = 8
DFT_BLOCK = 256
TOKEN_ROWS = D_MODEL // LANES
VMEM_LIMIT = 56 * 1024 * 1024

F32 = jnp.float32
BF16 = jnp.bfloat16


def _cparams(sem, **kw):
    return pltpu.CompilerParams(dimension_semantics=sem, vmem_limit_bytes=VMEM_LIMIT, **kw)


def _mod_kernel(c_ref, w_ref, b_ref, o_ref):
    c = c_ref[...]
    cond = c * jax.nn.sigmoid(c)
    o_ref[...] = jnp.dot(cond, w_ref[...], preferred_element_type=F32,
                         precision=lax.Precision.HIGHEST) + b_ref[...]


def _mod_call(crows, w_mod, b_mod):
    depth, d, nm = w_mod.shape
    tn = 1536
    return pl.pallas_call(
        _mod_kernel,
        grid=(depth, nm // tn),
        in_specs=[
            pl.BlockSpec((SUBLANES, d), lambda l, j: (0, 0)),
            pl.BlockSpec((None, d, tn), lambda l, j: (l, 0, j)),
            pl.BlockSpec((None, 1, tn), lambda l, j: (l, 0, j)),
        ],
        out_specs=pl.BlockSpec((None, SUBLANES, tn), lambda l, j: (l, 0, j)),
        out_shape=jax.ShapeDtypeStruct((depth, SUBLANES, nm), F32),
        compiler_params=_cparams(("arbitrary", "arbitrary")),
        name="mod",
    )(crows, w_mod, b_mod.reshape(depth, 1, nm))


def _rms_mod(x, g, shift, scale):
    y = x * lax.rsqrt(jnp.mean(x * x, axis=-1, keepdims=True) + EPS) * g
    return y * (1.0 + scale) + shift


def _proj_kernel(add_pos, *refs):
    if add_pos:
        x_ref, pos_ref, g_ref, sh_ref, sc_ref, w_ref, uf_ref, ux_ref, ug_ref, xp_ref, wbf_ref = refs
        x = x_ref[...] + pos_ref[...]
        xp_ref[...] = x
    else:
        x_ref, g_ref, sh_ref, sc_ref, w_ref, uf_ref, ux_ref, ug_ref, wbf_ref = refs
        x = x_ref[...]

    @pl.when((pl.program_id(0) == 0) & (pl.program_id(1) == 0))
    def _cast_w():
        wbf_ref[...] = w_ref[...].astype(BF16)

    h = _rms_mod(x, g_ref[...], sh_ref[...], sc_ref[...])
    u = jnp.dot(h.astype(BF16), wbf_ref[...], preferred_element_type=F32)
    uf_ref[...] = u[:, :D_FOURIER]
    ux_ref[...] = u[:, D_FOURIER:D_FOURIER + D_LRU]
    ug_ref[...] = u[:, D_FOURIER + D_LRU:]


def _proj_call(x, pos, g, shift, scale, w_in_bf):
    b, n, d = x.shape
    tm = min(512, n)
    add_pos = pos is not None
    row = pl.BlockSpec((None, tm, d), lambda i, j: (i, j, 0))
    vec_b = pl.BlockSpec((None, 1, d), lambda i, j: (i, 0, 0))
    in_specs = [row]
    args = [x]
    if add_pos:
        in_specs.append(pl.BlockSpec((tm, d), lambda i, j: (j, 0)))
        args.append(pos)
    in_specs += [pl.BlockSpec((1, d), lambda i, j: (0, 0)), vec_b, vec_b,
                 pl.BlockSpec((d, D_IN), lambda i, j: (0, 0))]
    args += [g.reshape(1, d), shift, scale, w_in_bf]
    part = pl.BlockSpec((None, tm, D_FOURIER), lambda i, j: (i, j, 0))
    out_specs = [part, part, part]
    out_shape = [jax.ShapeDtypeStruct((b, n, D_FOURIER), F32)] * 3
    if add_pos:
        out_specs.append(row)
        out_shape.append(jax.ShapeDtypeStruct((b, n, d), F32))
    return pl.pallas_call(
        functools.partial(_proj_kernel, add_pos),
        grid=(b, n // tm),
        in_specs=in_specs,
        out_specs=out_specs,
        out_shape=out_shape,
        scratch_shapes=[pltpu.VMEM((d, D_IN), BF16)],
        compiler_params=_cparams(("arbitrary", "arbitrary")),
        name="proj",
    )(*args)


def _cmul_const(re, im, c, s):
    tol = 1e-12
    if abs(s) < tol:
        if abs(c - 1.0) < tol:
            return re, im
        if abs(c + 1.0) < tol:
            return -re, -im
        return re * c, im * c
    if abs(c) < tol:
        if abs(s - 1.0) < tol:
            return -im, re
        if abs(s + 1.0) < tol:
            return im, -re
        return -im * s, re * s
    return re * c - im * s, re * s + im * c


def _fft_list(xs):
    n = len(xs)
    if n == 1:
        return xs
    ev = _fft_list(xs[0::2])
    od = _fft_list(xs[1::2])
    out = [None] * n
    for k in range(n // 2):
        ang = -2.0 * math.pi * k / n
        tr, ti = _cmul_const(od[k][0], od[k][1], math.cos(ang), math.sin(ang))
        out[k] = (ev[k][0] + tr, ev[k][1] + ti)
        out[k + n // 2] = (ev[k][0] - tr, ev[k][1] - ti)
    return out


def _fourier_kernel(n1_count, u_ref, cs_ref, twc_ref, tws_ref, cc_ref, o_ref, tr_ref, ti_ref, pq_ref):
    n = u_ref.shape[0]
    cs = cs_ref[...].astype(BF16)
    for n1 in range(n1_count):
        if n1_count == 1:
            z = u_ref[...]
        else:
            z = u_ref[pl.ds(n1, DFT_BLOCK, stride=n1_count), :]
        g = jnp.dot(cs, z.astype(BF16), preferred_element_type=F32)
        gr = g[:DFT_BLOCK]
        gs = g[DFT_BLOCK:]
        if n1 == 0:
            tr_ref[n1] = gr
            ti_ref[n1] = -gs
        else:
            c = twc_ref[n1]
            s = tws_ref[n1]
            tr_ref[n1] = gr * c - gs * s
            ti_ref[n1] = -(gr * s + gs * c)

    if n1_count == 1:
        pq_ref[:, :D_GROUP] = tr_ref[0]
        pq_ref[:, D_GROUP:] = -ti_ref[0]
    else:
        def chunk(j, carry):
            r0 = pl.multiple_of(j * SUBLANES, SUBLANES)
            xs = [(tr_ref[i, pl.ds(r0, SUBLANES), :], ti_ref[i, pl.ds(r0, SUBLANES), :])
                  for i in range(n1_count)]
            ys = _fft_list(xs)
            for k1 in range(n1_count):
                rows = pl.ds(pl.multiple_of(k1 * DFT_BLOCK + r0, SUBLANES), SUBLANES)
                pq_ref[rows, :D_GROUP] = ys[k1][0]
                pq_ref[rows, D_GROUP:] = -ys[k1][1]
            return carry
        lax.fori_loop(0, DFT_BLOCK // SUBLANES, chunk, 0)

    cc = cc_ref[...].astype(BF16)
    tile = min(512, n)
    for i in range(n // tile):
        rows = pl.ds(i * tile, tile)
        o_ref[rows, :] = jnp.dot(pq_ref[rows, :].astype(BF16), cc, preferred_element_type=F32)


@functools.lru_cache(maxsize=None)
def _fourier_consts(n):
    n1c = n // DFT_BLOCK
    k = np.arange(DFT_BLOCK, dtype=np.float64)
    ang = 2.0 * np.pi * np.outer(k, k) / DFT_BLOCK
    cs = np.concatenate([np.cos(ang), np.sin(ang)], axis=0)
    tw = 2.0 * np.pi * np.outer(np.arange(n1c, dtype=np.float64), k) / n
    twc = np.repeat(np.cos(tw)[:, :, None], LANES, axis=2)
    tws = np.repeat(np.sin(tw)[:, :, None], LANES, axis=2)
    c = np.arange(D_GROUP, dtype=np.float64)
    angc = 2.0 * np.pi * np.outer(c, c) / D_GROUP
    scale = 1.0 / math.sqrt(n * D_GROUP)
    cc = np.concatenate([np.cos(angc), -np.sin(angc)], axis=0) * scale
    return (cs.astype(np.float32), twc.astype(np.float32), tws.astype(np.float32), cc.astype(np.float32))


def _fourier_call(u):
    b, n, _ = u.shape
    n1c = n // DFT_BLOCK
    cs, twc, tws, cc = _fourier_consts(n)
    blk = pl.BlockSpec((None, n, D_GROUP), lambda i, j: (i, 0, j))
    return pl.pallas_call(
        functools.partial(_fourier_kernel, n1c),
        grid=(b, FOURIER_GROUPS),
        in_specs=[
            blk,
            pl.BlockSpec((2 * DFT_BLOCK, DFT_BLOCK), lambda i, j: (0, 0)),
            pl.BlockSpec((n1c, DFT_BLOCK, LANES), lambda i, j: (0, 0, 0)),
            pl.BlockSpec((n1c, DFT_BLOCK, LANES), lambda i, j: (0, 0, 0)),
            pl.BlockSpec((2 * D_GROUP, D_GROUP), lambda i, j: (0, 0)),
        ],
        out_specs=blk,
        out_shape=jax.ShapeDtypeStruct(u.shape, F32),
        scratch_shapes=[
            pltpu.VMEM((n1c, DFT_BLOCK, D_GROUP), F32),
            pltpu.VMEM((n1c, DFT_BLOCK, D_GROUP), F32),
            pltpu.VMEM((n, 2 * D_GROUP), F32),
        ],
        compiler_params=_cparams(("arbitrary", "arbitrary")),
        name="fourier",
    )(u, jnp.asarray(cs), jnp.asarray(twc), jnp.asarray(tws), jnp.asarray(cc))


SEGMENTS = SUBLANES
PAD_ROWS = 8


def _expm1(x):
    u = jnp.exp(x)
    near = (u - 1.0) * x / jnp.log(jnp.where(u == 1.0, 2.0, u))
    return jnp.where(x < -0.5, u - 1.0, jnp.where(u == 1.0, x, near))


def _lru_kernel(has_h0, *refs):
    if has_h0:
        (ux_ref, ug_ref, cw_ref, cb_ref, wg_ref, bg_ref, lam_ref, h0_ref,
         y_ref, fin_ref, upad_ref, a_ref, b_ref, h_ref) = refs
    else:
        (ux_ref, ug_ref, cw_ref, cb_ref, wg_ref, bg_ref, lam_ref,
         y_ref, fin_ref, upad_ref, a_ref, b_ref, h_ref) = refs
    n = ux_ref.shape[0]
    seg_len = n // SEGMENTS
    tile = min(256, seg_len)

    zeros_pad = jnp.zeros((PAD_ROWS, LANES), F32)
    upad_ref[pl.ds(0, PAD_ROWS), :] = zeros_pad
    upad_ref[pl.ds(PAD_ROWS + n, PAD_ROWS), :] = zeros_pad
    upad_ref[pl.ds(PAD_ROWS, n), :] = ux_ref[...]

    lam = lam_ref[...]
    sp = jnp.maximum(-lam, 0.0) + jnp.log1p(jnp.exp(-jnp.abs(lam)))
    cw = cw_ref[...]
    cb = cb_ref[...]
    wg = wg_ref[...].astype(BF16)
    bg = bg_ref[...]
    left = (CONV_W - 1) // 2

    for t in range(n // tile):
        r0 = t * tile
        xc = cb
        for k in range(CONV_W):
            xc = xc + upad_ref[pl.ds(PAD_ROWS + r0 + k - left, tile), :] * cw[k:k + 1, :]
        gates = jnp.dot(xc.astype(BF16), wg, preferred_element_type=F32) + bg
        seg = r0 // seg_len
        s0 = r0 - seg * seg_len
        for d in range(2):
            r = jax.nn.sigmoid(gates[:, (2 * d) * LANES:(2 * d + 1) * LANES])
            gi = jax.nn.sigmoid(gates[:, (2 * d + 1) * LANES:(2 * d + 2) * LANES])
            log_a = (-LRU_C) * r * sp[d:d + 1, :]
            a = jnp.exp(log_a)
            drive = jnp.sqrt(-_expm1(2.0 * log_a)) * (gi * xc)
            dst = pl.ds(s0 * SEGMENTS + seg, tile, stride=SEGMENTS)
            a_ref[d, dst, :] = a
            b_ref[d, dst, :] = drive

    def rows(s):
        return pl.ds(pl.multiple_of(s * SEGMENTS, SEGMENTS), SEGMENTS)

    def pass1(t, carry):
        hf, af, hb, ab = carry
        sb = seg_len - 1 - t
        a0 = a_ref[0, rows(t), :]
        a1 = a_ref[1, rows(sb), :]
        hf = a0 * hf + b_ref[0, rows(t), :]
        hb = a1 * hb + b_ref[1, rows(sb), :]
        return hf, af * a0, hb, ab * a1

    z = jnp.zeros((SEGMENTS, LANES), F32)
    o = jnp.ones((SEGMENTS, LANES), F32)
    hf, af, hb, ab = lax.fori_loop(0, seg_len, pass1, (z, o, z, o), unroll=8)

    if has_h0:
        cf = h0_ref[0:1, :]
        cbk = h0_ref[1:2, :]
    else:
        cf = jnp.zeros((1, LANES), F32)
        cbk = jnp.zeros((1, LANES), F32)
    sub = lax.broadcasted_iota(jnp.int32, (SEGMENTS, LANES), 0)
    carry_f = z
    for j in range(SEGMENTS):
        carry_f = jnp.where(sub == j, cf, carry_f)
        cf = hf[j:j + 1, :] + af[j:j + 1, :] * cf
    carry_b = z
    for j in reversed(range(SEGMENTS)):
        carry_b = jnp.where(sub == j, cbk, carry_b)
        cbk = hb[j:j + 1, :] + ab[j:j + 1, :] * cbk
    fin_ref[0:1, :] = cf
    fin_ref[1:2, :] = cbk

    def pass2(t, carry):
        hf, hb = carry
        sb = seg_len - 1 - t
        hf = a_ref[0, rows(t), :] * hf + b_ref[0, rows(t), :]
        h_ref[0, rows(t), :] = hf
        hb = a_ref[1, rows(sb), :] * hb + b_ref[1, rows(sb), :]
        h_ref[1, rows(sb), :] = hb
        return hf, hb

    lax.fori_loop(0, seg_len, pass2, (carry_f, carry_b), unroll=8)

    for t in range(n // tile):
        r0 = t * tile
        seg = r0 // seg_len
        s0 = r0 - seg * seg_len
        src = pl.ds(s0 * SEGMENTS + seg, tile, stride=SEGMENTS)
        hsum = h_ref[0, src, :] + h_ref[1, src, :]
        y_ref[pl.ds(r0, tile), :] = hsum * jax.nn.gelu(ug_ref[pl.ds(r0, tile), :])


def _lru_call(ux, ug, lp, h0):
    b, n, _ = ux.shape
    nblk = D_LRU // LANES
    has_h0 = h0 is not None
    blk = pl.BlockSpec((None, n, LANES), lambda i, j: (i, 0, j))
    in_specs = [
        blk, blk,
        pl.BlockSpec((CONV_W, LANES), lambda i, j: (0, j)),
        pl.BlockSpec((1, LANES), lambda i, j: (0, j)),
        pl.BlockSpec((None, LANES, 4 * LANES), lambda i, j: (j, 0, 0)),
        pl.BlockSpec((None, 1, 4 * LANES), lambda i, j: (j, 0, 0)),
        pl.BlockSpec((2, LANES), lambda i, j: (0, j)),
    ]
    args = [ux, ug, lp["conv_w"], lp["conv_b"], lp["w_gates"], lp["b_gates"], lp["lam"]]
    if has_h0:
        in_specs.append(pl.BlockSpec((None, 2, LANES), lambda i, j: (i, 0, j)))
        args.append(h0)
    return pl.pallas_call(
        functools.partial(_lru_kernel, has_h0),
        grid=(b, nblk),
        in_specs=in_specs,
        out_specs=[blk, pl.BlockSpec((None, 2, LANES), lambda i, j: (i, 0, j))],
        out_shape=[jax.ShapeDtypeStruct((b, n, D_LRU), F32),
                   jax.ShapeDtypeStruct((b, 2, D_LRU), F32)],
        scratch_shapes=[
            pltpu.VMEM((n + 2 * PAD_ROWS, LANES), F32),
            pltpu.VMEM((2, n, LANES), F32),
            pltpu.VMEM((2, n, LANES), F32),
            pltpu.VMEM((2, n, LANES), F32),
        ],
        compiler_params=_cparams(("arbitrary", "arbitrary")),
        name="lru",
    )(*args)


def _lru_params(conv_w, conv_b, wa, ba, wi, bi, lam):
    nblk = D_LRU // LANES
    hpb = LANES // LRU_HEAD_DIM

    def blockdiag(w):
        w = w.reshape(nblk, hpb, LRU_HEAD_DIM, LRU_HEAD_DIM)
        eye = jnp.eye(hpb, dtype=w.dtype)
        return jnp.einsum("bhij,hg->bhigj", w, eye).reshape(nblk, LANES, LANES)

    w_gates = jnp.concatenate([blockdiag(wa[0]), blockdiag(wi[0]), blockdiag(wa[1]), blockdiag(wi[1])], axis=-1)
    b_gates = jnp.stack([ba[0], bi[0], ba[1], bi[1]], axis=0)
    b_gates = b_gates.reshape(4, nblk, LANES).transpose(1, 0, 2).reshape(nblk, 1, 4 * LANES)
    return {"conv_w": conv_w, "conv_b": conv_b.reshape(1, D_LRU), "w_gates": w_gates,
            "b_gates": b_gates, "lam": lam}


def _mixout_kernel(x_ref, yf_ref, yr_ref, w_ref, g1_ref, n2_ref, sh_ref, sc_ref, wr_ref,
                   xo_ref, h2_ref, lg_ref, wbf_ref):
    @pl.when((pl.program_id(0) == 0) & (pl.program_id(1) == 0))
    def _cast_w():
        wbf_ref[...] = w_ref[...].astype(BF16)

    mix = jnp.dot(yf_ref[...].astype(BF16), wbf_ref[:D_FOURIER, :], preferred_element_type=F32)
    mix = mix + jnp.dot(yr_ref[...].astype(BF16), wbf_ref[D_FOURIER:, :], preferred_element_type=F32)
    x = x_ref[...] + g1_ref[...] * mix
    xo_ref[...] = x
    h2 = _rms_mod(x, n2_ref[...], sh_ref[...], sc_ref[...])
    tm = h2.shape[0]
    for s in range(TOKEN_ROWS):
        h2_ref[pl.ds(s, tm, stride=TOKEN_ROWS), :] = h2[:, s * LANES:(s + 1) * LANES]
    lg_ref[...] = lax.dot_general(wr_ref[...], h2, (((1,), (1,)), ((), ())),
                                  preferred_element_type=F32, precision=lax.Precision.HIGHEST)


def _mixout_call(x, yf, yr, w_out_bf, g1, n2g, sh2, sc2, w_router_t):
    b, n, d = x.shape
    tm = min(512, n)
    row = pl.BlockSpec((None, tm, d), lambda i, j: (i, j, 0))
    half = pl.BlockSpec((None, tm, D_FOURIER), lambda i, j: (i, j, 0))
    vec_b = pl.BlockSpec((None, 1, d), lambda i, j: (i, 0, 0))
    return pl.pallas_call(
        _mixout_kernel,
        grid=(b, n // tm),
        in_specs=[row, half, half,
                  pl.BlockSpec((d, d), lambda i, j: (0, 0)),
                  vec_b,
                  pl.BlockSpec((1, d), lambda i, j: (0, 0)),
                  vec_b, vec_b,
                  pl.BlockSpec((N_EXPERTS, d), lambda i, j: (0, 0))],
        out_specs=[row,
                   pl.BlockSpec((None, tm * TOKEN_ROWS, LANES), lambda i, j: (i, j, 0)),
                   pl.BlockSpec((None, N_EXPERTS, tm), lambda i, j: (i, 0, j))],
        out_shape=[jax.ShapeDtypeStruct((b, n, d), F32),
                   jax.ShapeDtypeStruct((b, n * TOKEN_ROWS, LANES), F32),
                   jax.ShapeDtypeStruct((b, N_EXPERTS, n), F32)],
        scratch_shapes=[pltpu.VMEM((d, d), BF16)],
        compiler_params=_cparams(("arbitrary", "arbitrary")),
        name="mixout",
    )(x, yf, yr, w_out_bf, g1, n2g.reshape(1, d), sh2, sc2, w_router_t)


def _lane_prefix_sum(x):
    n = x.shape[-1]
    lane = lax.broadcasted_iota(jnp.int32, x.shape, x.ndim - 1)
    s = 1
    while s < n:
        x = x + jnp.where(lane >= s, pltpu.roll(x, s, x.ndim - 1), 0)
        s *= 2
    return x


def _route_kernel(cap, lg_ref, key_ref, csum_ref, aff_ref, keyt_ref):
    n = lg_ref.shape[1]
    lg = lg_ref[...]
    m = jnp.max(lg, axis=0, keepdims=True)
    ex = jnp.exp(lg - m)
    aff = ex / jnp.sum(ex, axis=0, keepdims=True)
    aff_ref[...] = aff
    bits = pltpu.bitcast(aff, jnp.int32)

    def bis(_, carry):
        lo, hi = carry
        mid = lo + ((hi - lo + 1) >> 1)
        cnt = jnp.sum((bits >= mid).astype(jnp.int32), axis=1, keepdims=True)
        ok = cnt >= cap
        return jnp.where(ok, mid, lo), jnp.where(ok, hi, mid - 1)

    lo0 = jnp.zeros((N_EXPERTS, 1), jnp.int32)
    hi0 = jnp.full((N_EXPERTS, 1), 0x7F800000, jnp.int32)
    thr, _ = lax.fori_loop(0, 32, bis, (lo0, hi0))

    thr_f = pltpu.bitcast(thr, F32)
    window = 2.0 ** -6
    flo0 = thr_f * (1.0 - window)
    fhi0 = jnp.maximum(thr_f * (1.0 + window), jnp.float32(1e-37))

    def fbis(_, carry):
        lo, hi = carry
        mid = lo + 0.5 * (hi - lo)
        cnt = jnp.sum((aff >= mid).astype(jnp.int32), axis=1, keepdims=True)
        ok = cnt >= cap
        return jnp.where(ok, mid, lo), jnp.where(ok, hi, mid)

    flo, fhi = lax.fori_loop(0, 32, fbis, (flo0, fhi0))
    gt = aff >= fhi
    eq = (aff >= flo) & (aff < fhi)
    need = cap - jnp.sum(gt.astype(jnp.int32), axis=1, keepdims=True)
    eq_rank = _lane_prefix_sum(eq.astype(jnp.int32)) - eq.astype(jnp.int32)
    sel = gt | (eq & (eq_rank < need))
    csum = _lane_prefix_sum(sel.astype(jnp.int32))
    key = jnp.where(sel, csum - 1, -1)
    key_ref[...] = key
    csum_ref[...] = csum

    keyf = jnp.concatenate([key.astype(F32), jnp.full((LANES - N_EXPERTS, n), -1.0, F32)], axis=0)
    keyt_ref[...] = keyf.T


def _route_call(logits_t, cap):
    b, _, n = logits_t.shape
    en = pl.BlockSpec((None, N_EXPERTS, n), lambda i: (i, 0, 0))
    return pl.pallas_call(
        functools.partial(_route_kernel, cap),
        grid=(b,),
        in_specs=[en],
        out_specs=[en, en, en, pl.BlockSpec((None, n, LANES), lambda i: (i, 0, 0))],
        out_shape=[jax.ShapeDtypeStruct((b, N_EXPERTS, n), jnp.int32),
                   jax.ShapeDtypeStruct((b, N_EXPERTS, n), jnp.int32),
                   jax.ShapeDtypeStruct((b, N_EXPERTS, n), F32),
                   jax.ShapeDtypeStruct((b, n, LANES), F32)],
        compiler_params=_cparams(("arbitrary",)),
        name="route",
    )(logits_t)


def _compact_kernel(cap, klo_ref, khi_ref, key_ref, aff_ref, idx_ref, gate_ref):
    b = pl.program_id(0)
    nblk = key_ref.shape[1]
    rt = min(64, cap)
    ntile = cap // rt
    lane = lax.broadcasted_iota(jnp.int32, (1, LANES), 1)
    base = b * (nblk * LANES)
    zero = jnp.zeros((rt, LANES), F32)
    for e in range(N_EXPERTS):
        for t in range(ntile):
            s = (b * N_EXPERTS + e) * ntile + t
            rr = t * rt + lax.broadcasted_iota(jnp.int32, (rt, 1), 0)

            def block(k, carry, e=e, rr=rr):
                ids, gts = carry
                p = key_ref[e, pl.ds(k, 1), :] == rr
                tok = (base + k * LANES + lane).astype(F32)
                return (ids + jnp.where(p, tok, 0.0),
                        gts + jnp.where(p, aff_ref[e, pl.ds(k, 1), :], 0.0))

            ids, gts = lax.fori_loop(klo_ref[s], khi_ref[s], block, (zero, zero))
            idx_ref[e, t * rt:(t + 1) * rt, :] = jnp.sum(ids, axis=1, keepdims=True).astype(jnp.int32)
            gate_ref[e, t * rt:(t + 1) * rt, :] = jnp.sum(gts, axis=1, keepdims=True)


def _compact_call(key, csum, aff, cap):
    b, _, n = key.shape
    nblk = n // LANES
    rt = min(64, cap)
    ntile = cap // rt
    cb_incl = csum[:, :, LANES - 1::LANES]
    cb_excl = jnp.concatenate([jnp.zeros_like(cb_incl[..., :1]), cb_incl[..., :-1]], axis=-1)
    t0 = (jnp.arange(ntile, dtype=jnp.int32) * rt)[:, None]
    klo = jnp.sum((cb_incl[:, :, None, :] <= t0).astype(jnp.int32), axis=-1)
    khi = jnp.sum((cb_excl[:, :, None, :] < t0 + rt).astype(jnp.int32), axis=-1)
    blk = pl.BlockSpec((None, N_EXPERTS, nblk, LANES), lambda i, *_: (i, 0, 0, 0))
    out = pl.BlockSpec((None, N_EXPERTS, cap, 1), lambda i, *_: (i, 0, 0, 0))
    return pl.pallas_call(
        functools.partial(_compact_kernel, cap),
        grid_spec=pltpu.PrefetchScalarGridSpec(
            num_scalar_prefetch=2, grid=(b,), in_specs=[blk, blk], out_specs=[out, out]),
        out_shape=[jax.ShapeDtypeStruct((b, N_EXPERTS, cap, 1), jnp.int32),
                   jax.ShapeDtypeStruct((b, N_EXPERTS, cap, 1), F32)],
        compiler_params=_cparams(("arbitrary",)),
        name="compact",
    )(klo.reshape(-1), khi.reshape(-1), key.reshape(b, N_EXPERTS, nblk, LANES),
      aff.reshape(b, N_EXPERTS, nblk, LANES))


FF_TILE = 256


FF_STEPS = D_FF // FF_TILE
ROW_ALIGN = 32


def _round_up(v, m):
    return -(-v // m) * m


def _ffn_layout(row_counts):
    layout, off = [], 0
    for m in row_counts:
        padded = _round_up(m, FF_STEPS * SUBLANES)
        layout.append((m, padded, off))
        off = _round_up(off + padded, ROW_ALIGN)
    out_rows = _round_up(layout[-1][2] + layout[-1][0], ROW_ALIGN)
    scratch_rows = _round_up(layout[-1][2] + layout[-1][1], SUBLANES)
    return tuple(layout), out_rows, max(scratch_rows, out_rows)


def _ffn_kernel(layout, *refs):
    ns = len(layout)
    idx_refs = refs[:ns]
    gate_ref = refs[ns]
    tabs = refs[ns + 1:2 * ns + 1]
    wg_ref, wu_ref, wd_ref, o_ref, xg32_ref, xg_ref, acc_ref, sem = refs[2 * ns + 1:]
    e = pl.program_id(0)
    f = pl.program_id(1)
    n_e = pl.num_programs(0)
    slot = e % 2

    def issue_chunk(expert, dst_slot, step):
        for s, (_, padded, off) in enumerate(layout):
            chunk = padded // FF_STEPS
            src0 = expert * padded + step * chunk
            dst0 = off + step * chunk

            def body(g, carry, s=s, src0=src0, dst0=dst0):
                src = src0 + g * SUBLANES
                dst = pl.multiple_of(dst0 + g * SUBLANES, SUBLANES)
                for j in range(SUBLANES):
                    tok = idx_refs[s][src + j]
                    pltpu.make_async_copy(
                        tabs[s].at[pl.ds(pl.multiple_of(tok * TOKEN_ROWS, TOKEN_ROWS), TOKEN_ROWS), :],
                        xg32_ref.at[dst_slot, pl.ds(pl.multiple_of((dst + j) * TOKEN_ROWS, TOKEN_ROWS), TOKEN_ROWS), :],
                        sem.at[dst_slot]).start()
                return carry
            lax.fori_loop(0, chunk // SUBLANES, body, 0)

    @pl.when((e == 0) & (f == 0))
    def _first_expert():
        for step in range(FF_STEPS):
            issue_chunk(0, 0, step)

    @pl.when(f == 0)
    def _start_expert():
        for s, (m, padded, off) in enumerate(layout):
            pltpu.make_async_copy(tabs[s].at[pl.ds(0, padded * TOKEN_ROWS), :],
                                  xg32_ref.at[slot, pl.ds(off * TOKEN_ROWS, padded * TOKEN_ROWS), :],
                                  sem.at[slot]).wait()
        pack = 2 * SUBLANES
        for m, _, off in layout:
            def unpack(g, carry, off=off):
                t0 = pl.multiple_of(off + g * pack, pack)
                for c in range(TOKEN_ROWS):
                    v = xg32_ref[slot, pl.ds(t0 * TOKEN_ROWS + c, pack, stride=TOKEN_ROWS), :]
                    xg_ref[pl.ds(t0, pack), c * LANES:(c + 1) * LANES] = v.astype(BF16)
                return carry
            lax.fori_loop(0, m // pack, unpack, 0)
            acc_ref[pl.ds(off, m), :] = jnp.zeros((m, acc_ref.shape[1]), F32)

    @pl.when(e < n_e - 1)
    def _prefetch():
        issue_chunk(e + 1, 1 - slot, f)

    wg = wg_ref[...].astype(BF16)
    wu = wu_ref[...].astype(BF16)
    wd = wd_ref[...].astype(BF16)
    for m, _, off in layout:
        mt = min(512, m)
        for i in range(m // mt):
            rows = pl.ds(off + i * mt, mt)
            xg = xg_ref[rows, :]
            g = jnp.dot(xg, wg, preferred_element_type=F32)
            u = jnp.dot(xg, wu, preferred_element_type=F32)
            hid = (g * jax.nn.sigmoid(g) * u).astype(BF16)
            acc_ref[rows, :] += jnp.dot(hid, wd, preferred_element_type=F32)

    @pl.when(f == FF_STEPS - 1)
    def _finish():
        end = 0
        for m, _, off in layout:
            if off > end:
                o_ref[pl.ds(end, off - end), :] = jnp.zeros((off - end, o_ref.shape[1]), o_ref.dtype)
            o_ref[pl.ds(off, m), :] = (acc_ref[pl.ds(off, m), :] * gate_ref[pl.ds(off, m), :]).astype(o_ref.dtype)
            end = off + m
        if o_ref.shape[0] > end:
            o_ref[pl.ds(end, o_ref.shape[0] - end), :] = jnp.zeros((o_ref.shape[0] - end, o_ref.shape[1]), o_ref.dtype)


def _ffn_call(layer, row_counts, idxs, gate, tables, w_gate, w_up, w_down):
    ns = len(tables)
    n_e = w_gate.shape[1]
    d = w_gate.shape[2]
    layout, out_rows, scratch_rows = _ffn_layout(row_counts)
    assert gate.shape == (n_e, out_rows, 1)
    grid_spec = pltpu.PrefetchScalarGridSpec(
        num_scalar_prefetch=ns,
        grid=(n_e, FF_STEPS),
        in_specs=[pl.BlockSpec((None, out_rows, 1), lambda e, f, *_: (e, 0, 0))]
        + [pl.BlockSpec(memory_space=pl.ANY)] * ns
        + [pl.BlockSpec((None, None, d, FF_TILE), lambda e, f, *_: (layer, e, 0, f)),
           pl.BlockSpec((None, None, d, FF_TILE), lambda e, f, *_: (layer, e, 0, f)),
           pl.BlockSpec((None, None, FF_TILE, d), lambda e, f, *_: (layer, e, f, 0))],
        out_specs=pl.BlockSpec((None, out_rows, d), lambda e, f, *_: (e, 0, 0)),
        scratch_shapes=[pltpu.VMEM((2, scratch_rows * TOKEN_ROWS, LANES), F32), pltpu.VMEM((scratch_rows, d), BF16),
                        pltpu.VMEM((scratch_rows, d), F32), pltpu.SemaphoreType.DMA((2,))],
    )
    return pl.pallas_call(
        functools.partial(_ffn_kernel, layout),
        grid_spec=grid_spec,
        out_shape=jax.ShapeDtypeStruct((n_e, out_rows, d), BF16),
        compiler_params=_cparams(("arbitrary", "arbitrary")),
        name="ffn",
    )(*idxs, gate, *tables, w_gate, w_up, w_down)


COMBINE_TOKENS = 512
COMBINE_WINDOW = 256


def _combine_kernel(final, nwin, live_ref, *refs):
    if final:
        x_ref, kt_ref, y_ref, g2_ref, fg_ref, o_ref, acc_ref = refs
    else:
        x_ref, kt_ref, y_ref, g2_ref, o_ref, acc_ref = refs
    win = y_ref.shape[1] // nwin
    step = pl.program_id(0) * pl.num_programs(1) + pl.program_id(1)
    kt = kt_ref[...]
    lane = lax.broadcasted_iota(jnp.int32, (1, win), 1).astype(F32)
    acc_ref[...] = jnp.zeros(acc_ref.shape, F32)
    for e in range(N_EXPERTS):
        ke = kt[:, e:e + 1]
        for w in range(nwin):
            @pl.when(live_ref[(step * N_EXPERTS + e) * nwin + w] != 0)
            def _window(ke=ke, e=e, w=w):
                q = (ke == lane + float(w * win)).astype(BF16)
                acc_ref[...] += jnp.dot(q, y_ref[e, w * win:(w + 1) * win, :], preferred_element_type=F32)
    x = x_ref[...] + g2_ref[...] * acc_ref[...]
    if final:
        x = x * lax.rsqrt(jnp.mean(x * x, axis=-1, keepdims=True) + EPS) * fg_ref[...]
    o_ref[...] = x


def _combine_call(x, keyt, csum, yexp, first_row, cap, g2, final_g):
    b, n, d = x.shape
    tc = min(COMBINE_TOKENS, n)
    win = min(COMBINE_WINDOW, cap)
    nwin = cap // win
    final = final_g is not None
    blk0 = first_row // cap
    upto = csum[:, :, tc - 1::tc]
    before = jnp.concatenate([jnp.zeros_like(upto[..., :1]), upto[..., :-1]], axis=-1)
    w0 = (jnp.arange(nwin, dtype=jnp.int32) * win)
    live = (upto[..., None] > before[..., None]) & (before[..., None] < w0 + win) & (upto[..., None] > w0)
    live = live.transpose(0, 2, 1, 3).astype(jnp.int32).reshape(-1)
    row = pl.BlockSpec((None, tc, d), lambda i, j, *_: (i, j, 0))
    in_specs = [row,
                pl.BlockSpec((None, tc, LANES), lambda i, j, *_: (i, j, 0)),
                pl.BlockSpec((N_EXPERTS, cap, d), lambda i, j, *_: (0, blk0 + i, 0)),
                pl.BlockSpec((None, 1, d), lambda i, j, *_: (i, 0, 0))]
    args = [x, keyt, yexp, g2]
    if final:
        in_specs.append(pl.BlockSpec((1, d), lambda i, j, *_: (0, 0)))
        args.append(final_g.reshape(1, d))
    return pl.pallas_call(
        functools.partial(_combine_kernel, final, nwin),
        grid_spec=pltpu.PrefetchScalarGridSpec(
            num_scalar_prefetch=1, grid=(b, n // tc), in_specs=in_specs, out_specs=row,
            scratch_shapes=[pltpu.VMEM((tc, d), F32)]),
        out_shape=jax.ShapeDtypeStruct((b, n, d), F32),
        compiler_params=_cparams(("arbitrary", "arbitrary")),
        name="combine",
    )(live, *args)


def _grid_pos_embed(rows, d):
    rr, cc = np.meshgrid(np.arange(rows, dtype=np.float64), np.arange(GRID_W, dtype=np.float64), indexing="ij")
    quarter = d // 4
    omega = 1.0 / (POS_BASE ** (np.arange(quarter, dtype=np.float64) / quarter))

    def enc(p):
        ang = p.reshape(-1)[:, None] * omega
        return np.concatenate([np.sin(ang), np.cos(ang)], axis=-1)

    return np.concatenate([enc(rr), enc(cc)], axis=-1).astype(np.float32)


def _moe(layer, streams, w_gate, w_up, w_down):
    d = streams[0]["x"].shape[-1]
    caps = [EC_CAPACITY * s["x"].shape[1] // N_EXPERTS for s in streams]
    row_counts = [s["x"].shape[0] * cap for s, cap in zip(streams, caps)]
    layout, out_rows, _ = _ffn_layout(row_counts)
    idxs, keyts, csums = [], [], []
    gate_all = jnp.zeros((N_EXPERTS, out_rows, 1), F32)
    for s, cap, (m, padded, off) in zip(streams, caps, layout):
        b = s["x"].shape[0]
        key, csum, aff, keyt = _route_call(s["logits_t"], cap)
        idx, gate = _compact_call(key, csum, aff, cap)
        csums.append(csum)
        idx = idx.reshape(b, N_EXPERTS, cap).transpose(1, 0, 2).reshape(N_EXPERTS, m)
        idxs.append(jnp.pad(idx, ((0, 0), (0, padded - m))).reshape(N_EXPERTS * padded))
        gate = gate.reshape(b, N_EXPERTS, cap).transpose(1, 0, 2).reshape(N_EXPERTS, m, 1)
        gate_all = lax.dynamic_update_slice(gate_all, gate, (0, off, 0))
        keyts.append(keyt)
    tables = [s["h2"].reshape(-1, LANES) for s in streams]
    yexp = _ffn_call(layer, row_counts, idxs, gate_all, tables, w_gate, w_up, w_down)
    return [_combine_call(s["x"], keyt, csum, yexp, off, cap, s["g2"], s["final_g"])
            for s, keyt, csum, cap, (_, _, off) in zip(streams, keyts, csums, caps, layout)]


def kernel(x, c, ctx, c_ctx, w_mod, b_mod, norm1_g, norm2_g, w_in, conv_w, conv_b, lru_wa, lru_ba,
           lru_wi, lru_bi, lru_lambda, w_out, w_router, w_gate, w_up, w_down, final_g):
    bsz, n, d = x.shape
    depth = w_mod.shape[0]
    pos = jnp.asarray(_grid_pos_embed(n // GRID_W, d))

    crows = jnp.zeros((SUBLANES, d), F32).at[:bsz].set(c).at[bsz].set(c_ctx)
    mod = _mod_call(crows, w_mod, b_mod)

    for l in range(depth):
        last = l == depth - 1
        mx = mod[l, :bsz].reshape(bsz, 1, N_MOD, d)
        sh1, sc1, g1, sh2, sc2, g2 = [mx[:, :, i, :] for i in range(N_MOD)]
        mc = jnp.broadcast_to(mod[l, bsz].reshape(1, 1, N_MOD, d), (bsz, 1, N_MOD, d))
        csh1, csc1, cg1, csh2, csc2, cg2 = [mc[:, :, i, :] for i in range(N_MOD)]
        w_in_bf = w_in[l]
        w_out_bf = w_out[l]
        w_router_t = w_router[l].T
        lp = _lru_params(conv_w[l], conv_b[l], lru_wa[l], lru_ba[l], lru_wi[l], lru_bi[l], lru_lambda[l])

        cuf, cux, cug = _proj_call(ctx, None, norm1_g[l], csh1, csc1, w_in_bf)
        cyr, ctx_states = _lru_call(cux, cug, lp, None)
        streams = []
        if not last:
            cyf = _fourier_call(cuf)
            ctx, ch2, clg = _mixout_call(ctx, cyf, cyr, w_out_bf, cg1, norm2_g[l], csh2, csc2, w_router_t)
            streams.append(dict(x=ctx, h2=ch2, logits_t=clg, g2=cg2, final_g=None))

        if l == 0:
            uf, ux, ug, x = _proj_call(x, pos, norm1_g[l], sh1, sc1, w_in_bf)
        else:
            uf, ux, ug = _proj_call(x, None, norm1_g[l], sh1, sc1, w_in_bf)
        yf = _fourier_call(uf)
        yr, _ = _lru_call(ux, ug, lp, ctx_states)
        x, h2, lg = _mixout_call(x, yf, yr, w_out_bf, g1, norm2_g[l], sh2, sc2, w_router_t)
        streams.insert(0, dict(x=x, h2=h2, logits_t=lg, g2=g2, final_g=final_g if last else None))
        outs = _moe(l, streams, w_gate, w_up, w_down)
        x = outs[0]
        if not last:
            ctx = outs[1]
    return x
```

```python
import functools
import math

import numpy as np
import jax
import jax.numpy as jnp
from jax import lax
from jax.experimental import pallas as pl
from jax.experimental.pallas import tpu as pltpu

D_MODEL = 1024
GRID_W = 64
POS_BASE = 10000.0
D_FOURIER = 512
FOURIER_GROUPS = 4
D_GROUP = D_FOURIER // FOURIER_GROUPS
D_LRU = 512
LRU_HEADS = 8
LRU_HEAD_DIM = D_LRU // LRU_HEADS
CONV_W = 4
LRU_C = 8.0
D_IN = D_FOURIER + 2 * D_LRU
N_EXPERTS = 16
EC_CAPACITY = 2
D_FF = 2816
N_MOD = 6
EPS = 1e-6

LANES = 128
SUBLANES = 8
DFT_BLOCK = 256
TOKEN_ROWS = D_MODEL // LANES
VMEM_LIMIT = 56 * 1024 * 1024

F32 = jnp.float32
BF16 = jnp.bfloat16


def _cparams(sem, **kw):
    return pltpu.CompilerParams(dimension_semantics=sem, vmem_limit_bytes=VMEM_LIMIT, **kw)


def _mod_kernel(c_ref, w_ref, b_ref, o_ref):
    c = c_ref[...]
    cond = c * jax.nn.sigmoid(c)
    o_ref[...] = jnp.dot(cond, w_ref[...], preferred_element_type=F32,
                         precision=lax.Precision.HIGHEST) + b_ref[...]


def _mod_call(crows, w_mod, b_mod):
    depth, d, nm = w_mod.shape
    tn = 1536
    return pl.pallas_call(
        _mod_kernel,
        grid=(depth, nm // tn),
        in_specs=[
            pl.BlockSpec((SUBLANES, d), lambda l, j: (0, 0)),
            pl.BlockSpec((None, d, tn), lambda l, j: (l, 0, j)),
            pl.BlockSpec((None, 1, tn), lambda l, j: (l, 0, j)),
        ],
        out_specs=pl.BlockSpec((None, SUBLANES, tn), lambda l, j: (l, 0, j)),
        out_shape=jax.ShapeDtypeStruct((depth, SUBLANES, nm), F32),
        compiler_params=_cparams(("arbitrary", "arbitrary")),
        name="mod",
    )(crows, w_mod, b_mod.reshape(depth, 1, nm))


def _rms_mod(x, g, shift, scale):
    y = x * lax.rsqrt(jnp.mean(x * x, axis=-1, keepdims=True) + EPS) * g
    return y * (1.0 + scale) + shift


def _proj_kernel(add_pos, *refs):
    if add_pos:
        x_ref, pos_ref, g_ref, sh_ref, sc_ref, w_ref, uf_ref, ux_ref, ug_ref, xp_ref, wbf_ref = refs
        x = x_ref[...] + pos_ref[...]
        xp_ref[...] = x
    else:
        x_ref, g_ref, sh_ref, sc_ref, w_ref, uf_ref, ux_ref, ug_ref, wbf_ref = refs
        x = x_ref[...]

    @pl.when((pl.program_id(0) == 0) & (pl.program_id(1) == 0))
    def _cast_w():
        wbf_ref[...] = w_ref[...].astype(BF16)

    h = _rms_mod(x, g_ref[...], sh_ref[...], sc_ref[...])
    u = jnp.dot(h.astype(BF16), wbf_ref[...], preferred_element_type=F32)
    uf_ref[...] = u[:, :D_FOURIER]
    ux_ref[...] = u[:, D_FOURIER:D_FOURIER + D_LRU]
    ug_ref[...] = u[:, D_FOURIER + D_LRU:]


def _proj_call(x, pos, g, shift, scale, w_in_bf):
    b, n, d = x.shape
    tm = min(512, n)
    add_pos = pos is not None
    row = pl.BlockSpec((None, tm, d), lambda i, j: (i, j, 0))
    vec_b = pl.BlockSpec((None, 1, d), lambda i, j: (i, 0, 0))
    in_specs = [row]
    args = [x]
    if add_pos:
        in_specs.append(pl.BlockSpec((tm, d), lambda i, j: (j, 0)))
        args.append(pos)
    in_specs += [pl.BlockSpec((1, d), lambda i, j: (0, 0)), vec_b, vec_b,
                 pl.BlockSpec((d, D_IN), lambda i, j: (0, 0))]
    args += [g.reshape(1, d), shift, scale, w_in_bf]
    part = pl.BlockSpec((None, tm, D_FOURIER), lambda i, j: (i, j, 0))
    out_specs = [part, part, part]
    out_shape = [jax.ShapeDtypeStruct((b, n, D_FOURIER), F32)] * 3
    if add_pos:
        out_specs.append(row)
        out_shape.append(jax.ShapeDtypeStruct((b, n, d), F32))
    return pl.pallas_call(
        functools.partial(_proj_kernel, add_pos),
        grid=(b, n // tm),
        in_specs=in_specs,
        out_specs=out_specs,
        out_shape=out_shape,
        scratch_shapes=[pltpu.VMEM((d, D_IN), BF16)],
        compiler_params=_cparams(("arbitrary", "arbitrary")),
        name="proj",
    )(*args)


def _cmul_const(re, im, c, s):
    tol = 1e-12
    if abs(s) < tol:
        if abs(c - 1.0) < tol:
            return re, im
        if abs(c + 1.0) < tol:
            return -re, -im
        return re * c, im * c
    if abs(c) < tol:
        if abs(s - 1.0) < tol:
            return -im, re
        if abs(s + 1.0) < tol:
            return im, -re
        return -im * s, re * s
    return re * c - im * s, re * s + im * c


def _fft_list(xs):
    n = len(xs)
    if n == 1:
        return xs
    ev = _fft_list(xs[0::2])
    od = _fft_list(xs[1::2])
    out = [None] * n
    for k in range(n // 2):
        ang = -2.0 * math.pi * k / n
        tr, ti = _cmul_const(od[k][0], od[k][1], math.cos(ang), math.sin(ang))
        out[k] = (ev[k][0] + tr, ev[k][1] + ti)
        out[k + n // 2] = (ev[k][0] - tr, ev[k][1] - ti)
    return out


def _fourier_kernel(n1_count, u_ref, cs_ref, twc_ref, tws_ref, cc_ref, o_ref, tr_ref, ti_ref, pq_ref):
    n = u_ref.shape[0]
    cs = cs_ref[...].astype(BF16)
    for n1 in range(n1_count):
        if n1_count == 1:
            z = u_ref[...]
        else:
            z = u_ref[pl.ds(n1, DFT_BLOCK, stride=n1_count), :]
        g = jnp.dot(cs, z.astype(BF16), preferred_element_type=F32)
        gr = g[:DFT_BLOCK]
        gs = g[DFT_BLOCK:]
        if n1 == 0:
            tr_ref[n1] = gr
            ti_ref[n1] = -gs
        else:
            c = twc_ref[n1]
            s = tws_ref[n1]
            tr_ref[n1] = gr * c - gs * s
            ti_ref[n1] = -(gr * s + gs * c)

    if n1_count == 1:
        pq_ref[:, :D_GROUP] = tr_ref[0]
        pq_ref[:, D_GROUP:] = -ti_ref[0]
    else:
        def chunk(j, carry):
            r0 = pl.multiple_of(j * SUBLANES, SUBLANES)
            xs = [(tr_ref[i, pl.ds(r0, SUBLANES), :], ti_ref[i, pl.ds(r0, SUBLANES), :])
                  for i in range(n1_count)]
            ys = _fft_list(xs)
            for k1 in range(n1_count):
                rows = pl.ds(pl.multiple_of(k1 * DFT_BLOCK + r0, SUBLANES), SUBLANES)
                pq_ref[rows, :D_GROUP] = ys[k1][0]
                pq_ref[rows, D_GROUP:] = -ys[k1][1]
            return carry
        lax.fori_loop(0, DFT_BLOCK // SUBLANES, chunk, 0)

    cc = cc_ref[...].astype(BF16)
    tile = min(512, n)
    for i in range(n // tile):
        rows = pl.ds(i * tile, tile)
        o_ref[rows, :] = jnp.dot(pq_ref[rows, :].astype(BF16), cc, preferred_element_type=F32)


@functools.lru_cache(maxsize=None)
def _fourier_consts(n):
    n1c = n // DFT_BLOCK
    k = np.arange(DFT_BLOCK, dtype=np.float64)
    ang = 2.0 * np.pi * np.outer(k, k) / DFT_BLOCK
    cs = np.concatenate([np.cos(ang), np.sin(ang)], axis=0)
    tw = 2.0 * np.pi * np.outer(np.arange(n1c, dtype=np.float64), k) / n
    twc = np.repeat(np.cos(tw)[:, :, None], LANES, axis=2)
    tws = np.repeat(np.sin(tw)[:, :, None], LANES, axis=2)
    c = np.arange(D_GROUP, dtype=np.float64)
    angc = 2.0 * np.pi * np.outer(c, c) / D_GROUP
    scale = 1.0 / math.sqrt(n * D_GROUP)
    cc = np.concatenate([np.cos(angc), -np.sin(angc)], axis=0) * scale
    return (cs.astype(np.float32), twc.astype(np.float32), tws.astype(np.float32), cc.astype(np.float32))


def _fourier_call(u):
    b, n, _ = u.shape
    n1c = n // DFT_BLOCK
    cs, twc, tws, cc = _fourier_consts(n)
    blk = pl.BlockSpec((None, n, D_GROUP), lambda i, j: (i, 0, j))
    return pl.pallas_call(
        functools.partial(_fourier_kernel, n1c),
        grid=(b, FOURIER_GROUPS),
        in_specs=[
            blk,
            pl.BlockSpec((2 * DFT_BLOCK, DFT_BLOCK), lambda i, j: (0, 0)),
            pl.BlockSpec((n1c, DFT_BLOCK, LANES), lambda i, j: (0, 0, 0)),
            pl.BlockSpec((n1c, DFT_BLOCK, LANES), lambda i, j: (0, 0, 0)),
            pl.BlockSpec((2 * D_GROUP, D_GROUP), lambda i, j: (0, 0)),
        ],
        out_specs=blk,
        out_shape=jax.ShapeDtypeStruct(u.shape, F32),
        scratch_shapes=[
            pltpu.VMEM((n1c, DFT_BLOCK, D_GROUP), F32),
            pltpu.VMEM((n1c, DFT_BLOCK, D_GROUP), F32),
            pltpu.VMEM((n, 2 * D_GROUP), F32),
        ],
        compiler_params=_cparams(("arbitrary", "arbitrary")),
        name="fourier",
    )(u, jnp.asarray(cs), jnp.asarray(twc), jnp.asarray(tws), jnp.asarray(cc))


SEGMENTS = SUBLANES
PAD_ROWS = 8


def _expm1(x):
    u = jnp.exp(x)
    near = (u - 1.0) * x / jnp.log(jnp.where(u == 1.0, 2.0, u))
    return jnp.where(x < -0.5, u - 1.0, jnp.where(u == 1.0, x, near))


def _lru_kernel(has_h0, *refs):
    if has_h0:
        (ux_ref, ug_ref, cw_ref, cb_ref, wg_ref, bg_ref, lam_ref, h0_ref,
         y_ref, fin_ref, upad_ref, a_ref, b_ref, h_ref) = refs
    else:
        (ux_ref, ug_ref, cw_ref, cb_ref, wg_ref, bg_ref, lam_ref,
         y_ref, fin_ref, upad_ref, a_ref, b_ref, h_ref) = refs
    n = ux_ref.shape[0]
    seg_len = n // SEGMENTS
    tile = min(256, seg_len)

    zeros_pad = jnp.zeros((PAD_ROWS, LANES), F32)
    upad_ref[pl.ds(0, PAD_ROWS), :] = zeros_pad
    upad_ref[pl.ds(PAD_ROWS + n, PAD_ROWS), :] = zeros_pad
    upad_ref[pl.ds(PAD_ROWS, n), :] = ux_ref[...]

    lam = lam_ref[...]
    sp = jnp.maximum(-lam, 0.0) + jnp.log1p(jnp.exp(-jnp.abs(lam)))
    cw = cw_ref[...]
    cb = cb_ref[...]
    wg = wg_ref[...].astype(BF16)
    bg = bg_ref[...]
    left = (CONV_W - 1) // 2

    for t in range(n // tile):
        r0 = t * tile
        xc = cb
        for k in range(CONV_W):
            xc = xc + upad_ref[pl.ds(PAD_ROWS + r0 + k - left, tile), :] * cw[k:k + 1, :]
        gates = jnp.dot(xc.astype(BF16), wg, preferred_element_type=F32) + bg
        seg = r0 // seg_len
        s0 = r0 - seg * seg_len
        for d in range(2):
            r = jax.nn.sigmoid(gates[:, (2 * d) * LANES:(2 * d + 1) * LANES])
            gi = jax.nn.sigmoid(gates[:, (2 * d + 1) * LANES:(2 * d + 2) * LANES])
            log_a = (-LRU_C) * r * sp[d:d + 1, :]
            a = jnp.exp(log_a)
            drive = jnp.sqrt(-_expm1(2.0 * log_a)) * (gi * xc)
            dst = pl.ds(s0 * SEGMENTS + seg, tile, stride=SEGMENTS)
            a_ref[d, dst, :] = a
            b_ref[d, dst, :] = drive

    def rows(s):
        return pl.ds(pl.multiple_of(s * SEGMENTS, SEGMENTS), SEGMENTS)

    def pass1(t, carry):
        hf, af, hb, ab = carry
        sb = seg_len - 1 - t
        a0 = a_ref[0, rows(t), :]
        a1 = a_ref[1, rows(sb), :]
        hf = a0 * hf + b_ref[0, rows(t), :]
        hb = a1 * hb + b_ref[1, rows(sb), :]
        return hf, af * a0, hb, ab * a1

    z = jnp.zeros((SEGMENTS, LANES), F32)
    o = jnp.ones((SEGMENTS, LANES), F32)
    hf, af, hb, ab = lax.fori_loop(0, seg_len, pass1, (z, o, z, o), unroll=8)

    if has_h0:
        cf = h0_ref[0:1, :]
        cbk = h0_ref[1:2, :]
    else:
        cf = jnp.zeros((1, LANES), F32)
        cbk = jnp.zeros((1, LANES), F32)
    sub = lax.broadcasted_iota(jnp.int32, (SEGMENTS, LANES), 0)
    carry_f = z
    for j in range(SEGMENTS):
        carry_f = jnp.where(sub == j, cf, carry_f)
        cf = hf[j:j + 1, :] + af[j:j + 1, :] * cf
    carry_b = z
    for j in reversed(range(SEGMENTS)):
        carry_b = jnp.where(sub == j, cbk, carry_b)
        cbk = hb[j:j + 1, :] + ab[j:j + 1, :] * cbk
    fin_ref[0:1, :] = cf
    fin_ref[1:2, :] = cbk

    def pass2(t, carry):
        hf, hb = carry
        sb = seg_len - 1 - t
        hf = a_ref[0, rows(t), :] * hf + b_ref[0, rows(t), :]
        h_ref[0, rows(t), :] = hf
        hb = a_ref[1, rows(sb), :] * hb + b_ref[1, rows(sb), :]
        h_ref[1, rows(sb), :] = hb
        return hf, hb

    lax.fori_loop(0, seg_len, pass2, (carry_f, carry_b), unroll=8)

    for t in range(n // tile):
        r0 = t * tile
        seg = r0 // seg_len
        s0 = r0 - seg * seg_len
        src = pl.ds(s0 * SEGMENTS + seg, tile, stride=SEGMENTS)
        hsum = h_ref[0, src, :] + h_ref[1, src, :]
        y_ref[pl.ds(r0, tile), :] = hsum * jax.nn.gelu(ug_ref[pl.ds(r0, tile), :])


def _lru_call(ux, ug, lp, h0):
    b, n, _ = ux.shape
    nblk = D_LRU // LANES
    has_h0 = h0 is not None
    blk = pl.BlockSpec((None, n, LANES), lambda i, j: (i, 0, j))
    in_specs = [
        blk, blk,
        pl.BlockSpec((CONV_W, LANES), lambda i, j: (0, j)),
        pl.BlockSpec((1, LANES), lambda i, j: (0, j)),
        pl.BlockSpec((None, LANES, 4 * LANES), lambda i, j: (j, 0, 0)),
        pl.BlockSpec((None, 1, 4 * LANES), lambda i, j: (j, 0, 0)),
        pl.BlockSpec((2, LANES), lambda i, j: (0, j)),
    ]
    args = [ux, ug, lp["conv_w"], lp["conv_b"], lp["w_gates"], lp["b_gates"], lp["lam"]]
    if has_h0:
        in_specs.append(pl.BlockSpec((None, 2, LANES), lambda i, j: (i, 0, j)))
        args.append(h0)
    return pl.pallas_call(
        functools.partial(_lru_kernel, has_h0),
        grid=(b, nblk),
        in_specs=in_specs,
        out_specs=[blk, pl.BlockSpec((None, 2, LANES), lambda i, j: (i, 0, j))],
        out_shape=[jax.ShapeDtypeStruct((b, n, D_LRU), F32),
                   jax.ShapeDtypeStruct((b, 2, D_LRU), F32)],
        scratch_shapes=[
            pltpu.VMEM((n + 2 * PAD_ROWS, LANES), F32),
            pltpu.VMEM((2, n, LANES), F32),
            pltpu.VMEM((2, n, LANES), F32),
            pltpu.VMEM((2, n, LANES), F32),
        ],
        compiler_params=_cparams(("arbitrary", "arbitrary")),
        name="lru",
    )(*args)


def _lru_params(conv_w, conv_b, wa, ba, wi, bi, lam):
    nblk = D_LRU // LANES
    hpb = LANES // LRU_HEAD_DIM

    def blockdiag(w):
        w = w.reshape(nblk, hpb, LRU_HEAD_DIM, LRU_HEAD_DIM)
        eye = jnp.eye(hpb, dtype=w.dtype)
        return jnp.einsum("bhij,hg->bhigj", w, eye).reshape(nblk, LANES, LANES)

    w_gates = jnp.concatenate([blockdiag(wa[0]), blockdiag(wi[0]), blockdiag(wa[1]), blockdiag(wi[1])], axis=-1)
    b_gates = jnp.stack([ba[0], bi[0], ba[1], bi[1]], axis=0)
    b_gates = b_gates.reshape(4, nblk, LANES).transpose(1, 0, 2).reshape(nblk, 1, 4 * LANES)
    return {"conv_w": conv_w, "conv_b": conv_b.reshape(1, D_LRU), "w_gates": w_gates,
            "b_gates": b_gates, "lam": lam}


def _mixout_kernel(x_ref, yf_ref, yr_ref, w_ref, g1_ref, n2_ref, sh_ref, sc_ref, wr_ref,
                   xo_ref, h2_ref, lg_ref, wbf_ref):
    @pl.when((pl.program_id(0) == 0) & (pl.program_id(1) == 0))
    def _cast_w():
        wbf_ref[...] = w_ref[...].astype(BF16)

    mix = jnp.dot(yf_ref[...].astype(BF16), wbf_ref[:D_FOURIER, :], preferred_element_type=F32)
    mix = mix + jnp.dot(yr_ref[...].astype(BF16), wbf_ref[D_FOURIER:, :], preferred_element_type=F32)
    x = x_ref[...] + g1_ref[...] * mix
    xo_ref[...] = x
    h2 = _rms_mod(x, n2_ref[...], sh_ref[...], sc_ref[...])
    tm = h2.shape[0]
    for s in range(TOKEN_ROWS):
        h2_ref[pl.ds(s, tm, stride=TOKEN_ROWS), :] = h2[:, s * LANES:(s + 1) * LANES]
    lg_ref[...] = lax.dot_general(wr_ref[...], h2, (((1,), (1,)), ((), ())),
                                  preferred_element_type=F32, precision=lax.Precision.HIGHEST)


def _mixout_call(x, yf, yr, w_out_bf, g1, n2g, sh2, sc2, w_router_t):
    b, n, d = x.shape
    tm = min(512, n)
    row = pl.BlockSpec((None, tm, d), lambda i, j: (i, j, 0))
    half = pl.BlockSpec((None, tm, D_FOURIER), lambda i, j: (i, j, 0))
    vec_b = pl.BlockSpec((None, 1, d), lambda i, j: (i, 0, 0))
    return pl.pallas_call(
        _mixout_kernel,
        grid=(b, n // tm),
        in_specs=[row, half, half,
                  pl.BlockSpec((d, d), lambda i, j: (0, 0)),
                  vec_b,
                  pl.BlockSpec((1, d), lambda i, j: (0, 0)),
                  vec_b, vec_b,
                  pl.BlockSpec((N_EXPERTS, d), lambda i, j: (0, 0))],
        out_specs=[row,
                   pl.BlockSpec((tm * TOKEN_ROWS, LANES), lambda i, j: (i * (n // tm) + j, 0)),
                   pl.BlockSpec((None, N_EXPERTS, tm), lambda i, j: (i, 0, j))],
        out_shape=[jax.ShapeDtypeStruct((b, n, d), F32),
                   jax.ShapeDtypeStruct((b * n * TOKEN_ROWS, LANES), F32),
                   jax.ShapeDtypeStruct((b, N_EXPERTS, n), F32)],
        scratch_shapes=[pltpu.VMEM((d, d), BF16)],
        compiler_params=_cparams(("arbitrary", "arbitrary")),
        name="mixout",
    )(x, yf, yr, w_out_bf, g1, n2g.reshape(1, d), sh2, sc2, w_router_t)


def _lane_prefix_sum(x):
    n = x.shape[-1]
    lane = lax.broadcasted_iota(jnp.int32, x.shape, x.ndim - 1)
    s = 1
    while s < n:
        x = x + jnp.where(lane >= s, pltpu.roll(x, s, x.ndim - 1), 0)
        s *= 2
    return x


def _route_kernel(cap, lg_ref, key_ref, csum_ref, aff_ref, keyt_ref):
    n = lg_ref.shape[1]
    lg = lg_ref[...]
    m = jnp.max(lg, axis=0, keepdims=True)
    ex = jnp.exp(lg - m)
    aff = ex / jnp.sum(ex, axis=0, keepdims=True)
    aff_ref[...] = aff
    bits = pltpu.bitcast(aff, jnp.int32)

    def bis(_, carry):
        lo, hi = carry
        mid = lo + ((hi - lo + 1) >> 1)
        cnt = jnp.sum((bits >= mid).astype(jnp.int32), axis=1, keepdims=True)
        ok = cnt >= cap
        return jnp.where(ok, mid, lo), jnp.where(ok, hi, mid - 1)

    lo0 = jnp.zeros((N_EXPERTS, 1), jnp.int32)
    hi0 = jnp.full((N_EXPERTS, 1), 0x7F800000, jnp.int32)
    thr, _ = lax.fori_loop(0, 32, bis, (lo0, hi0))

    thr_f = pltpu.bitcast(thr, F32)
    window = 2.0 ** -6
    flo0 = thr_f * (1.0 - window)
    fhi0 = jnp.maximum(thr_f * (1.0 + window), jnp.float32(1e-37))

    def fbis(_, carry):
        lo, hi = carry
        mid = lo + 0.5 * (hi - lo)
        cnt = jnp.sum((aff >= mid).astype(jnp.int32), axis=1, keepdims=True)
        ok = cnt >= cap
        return jnp.where(ok, mid, lo), jnp.where(ok, hi, mid)

    flo, fhi = lax.fori_loop(0, 32, fbis, (flo0, fhi0))
    gt = aff >= fhi
    eq = (aff >= flo) & (aff < fhi)
    need = cap - jnp.sum(gt.astype(jnp.int32), axis=1, keepdims=True)
    eq_rank = _lane_prefix_sum(eq.astype(jnp.int32)) - eq.astype(jnp.int32)
    sel = gt | (eq & (eq_rank < need))
    csum = _lane_prefix_sum(sel.astype(jnp.int32))
    key = jnp.where(sel, csum - 1, -1)
    key_ref[...] = key
    csum_ref[...] = csum

    keyf = jnp.concatenate([key.astype(F32), jnp.full((LANES - N_EXPERTS, n), -1.0, F32)], axis=0)
    keyt_ref[...] = keyf.T


def _route_call(logits_t, cap):
    b, _, n = logits_t.shape
    en = pl.BlockSpec((None, N_EXPERTS, n), lambda i: (i, 0, 0))
    return pl.pallas_call(
        functools.partial(_route_kernel, cap),
        grid=(b,),
        in_specs=[en],
        out_specs=[en, en, en, pl.BlockSpec((None, n, LANES), lambda i: (i, 0, 0))],
        out_shape=[jax.ShapeDtypeStruct((b, N_EXPERTS, n), jnp.int32),
                   jax.ShapeDtypeStruct((b, N_EXPERTS, n), jnp.int32),
                   jax.ShapeDtypeStruct((b, N_EXPERTS, n), F32),
                   jax.ShapeDtypeStruct((b, n, LANES), F32)],
        compiler_params=_cparams(("arbitrary",)),
        name="route",
    )(logits_t)


def _compact_kernel(cap, klo_ref, khi_ref, key_ref, aff_ref, idx_ref, gate_ref):
    b = pl.program_id(0)
    nblk = key_ref.shape[1]
    rt = min(64, cap)
    ntile = cap // rt
    lane = lax.broadcasted_iota(jnp.int32, (1, LANES), 1)
    base = b * (nblk * LANES)
    zero = jnp.zeros((rt, LANES), F32)
    for e in range(N_EXPERTS):
        for t in range(ntile):
            s = (b * N_EXPERTS + e) * ntile + t
            rr = t * rt + lax.broadcasted_iota(jnp.int32, (rt, 1), 0)

            def block(k, carry, e=e, rr=rr):
                ids, gts = carry
                p = key_ref[e, pl.ds(k, 1), :] == rr
                tok = (base + k * LANES + lane).astype(F32)
                return (ids + jnp.where(p, tok, 0.0),
                        gts + jnp.where(p, aff_ref[e, pl.ds(k, 1), :], 0.0))

            ids, gts = lax.fori_loop(klo_ref[s], khi_ref[s], block, (zero, zero))
            idx_ref[e, t * rt:(t + 1) * rt, :] = jnp.sum(ids, axis=1, keepdims=True).astype(jnp.int32)
            gate_ref[e, t * rt:(t + 1) * rt, :] = jnp.sum(gts, axis=1, keepdims=True)


def _compact_call(key, csum, aff, cap):
    b, _, n = key.shape
    nblk = n // LANES
    rt = min(64, cap)
    ntile = cap // rt
    cb_incl = csum[:, :, LANES - 1::LANES]
    cb_excl = jnp.concatenate([jnp.zeros_like(cb_incl[..., :1]), cb_incl[..., :-1]], axis=-1)
    t0 = (jnp.arange(ntile, dtype=jnp.int32) * rt)[:, None]
    klo = jnp.sum((cb_incl[:, :, None, :] <= t0).astype(jnp.int32), axis=-1)
    khi = jnp.sum((cb_excl[:, :, None, :] < t0 + rt).astype(jnp.int32), axis=-1)
    blk = pl.BlockSpec((None, N_EXPERTS, nblk, LANES), lambda i, *_: (i, 0, 0, 0))
    out = pl.BlockSpec((None, N_EXPERTS, cap, 1), lambda i, *_: (i, 0, 0, 0))
    return pl.pallas_call(
        functools.partial(_compact_kernel, cap),
        grid_spec=pltpu.PrefetchScalarGridSpec(
            num_scalar_prefetch=2, grid=(b,), in_specs=[blk, blk], out_specs=[out, out]),
        out_shape=[jax.ShapeDtypeStruct((b, N_EXPERTS, cap, 1), jnp.int32),
                   jax.ShapeDtypeStruct((b, N_EXPERTS, cap, 1), F32)],
        compiler_params=_cparams(("arbitrary",)),
        name="compact",
    )(klo.reshape(-1), khi.reshape(-1), key.reshape(b, N_EXPERTS, nblk, LANES),
      aff.reshape(b, N_EXPERTS, nblk, LANES))


FF_TILE = 256


FF_STEPS = D_FF // FF_TILE
ROW_ALIGN = 32


def _round_up(v, m):
    return -(-v // m) * m


def _ffn_layout(row_counts):
    layout, off = [], 0
    for m in row_counts:
        padded = _round_up(m, FF_STEPS * SUBLANES)
        layout.append((m, padded, off))
        off = _round_up(off + padded, ROW_ALIGN)
    out_rows = _round_up(layout[-1][2] + layout[-1][0], ROW_ALIGN)
    scratch_rows = _round_up(layout[-1][2] + layout[-1][1], SUBLANES)
    return tuple(layout), out_rows, max(scratch_rows, out_rows)


def _ffn_kernel(layout, *refs):
    ns = len(layout)
    idx_refs = refs[:ns]
    gate_ref = refs[ns]
    tabs = refs[ns + 1:2 * ns + 1]
    wg_ref, wu_ref, wd_ref, o_ref, xg32_ref, xg_ref, acc_ref, sem = refs[2 * ns + 1:]
    e = pl.program_id(0)
    f = pl.program_id(1)
    n_e = pl.num_programs(0)
    slot = e % 2

    def issue_chunk(expert, dst_slot, step):
        for s, (_, padded, off) in enumerate(layout):
            chunk = padded // FF_STEPS
            src0 = expert * padded + step * chunk
            dst0 = off + step * chunk

            def body(g, carry, s=s, src0=src0, dst0=dst0):
                src = src0 + g * SUBLANES
                dst = pl.multiple_of(dst0 + g * SUBLANES, SUBLANES)
                for j in range(SUBLANES):
                    tok = idx_refs[s][src + j]
                    pltpu.make_async_copy(
                        tabs[s].at[pl.ds(pl.multiple_of(tok * TOKEN_ROWS, TOKEN_ROWS), TOKEN_ROWS), :],
                        xg32_ref.at[dst_slot, pl.ds(pl.multiple_of((dst + j) * TOKEN_ROWS, TOKEN_ROWS), TOKEN_ROWS), :],
                        sem.at[dst_slot]).start()
                return carry
            lax.fori_loop(0, chunk // SUBLANES, body, 0)

    @pl.when((e == 0) & (f == 0))
    def _first_expert():
        for step in range(FF_STEPS):
            issue_chunk(0, 0, step)

    @pl.when(f == 0)
    def _start_expert():
        for s, (m, padded, off) in enumerate(layout):
            pltpu.make_async_copy(tabs[s].at[pl.ds(0, padded * TOKEN_ROWS), :],
                                  xg32_ref.at[slot, pl.ds(off * TOKEN_ROWS, padded * TOKEN_ROWS), :],
                                  sem.at[slot]).wait()
        pack = 2 * SUBLANES
        for m, _, off in layout:
            def unpack(g, carry, off=off):
                t0 = pl.multiple_of(off + g * pack, pack)
                for c in range(TOKEN_ROWS):
                    v = xg32_ref[slot, pl.ds(t0 * TOKEN_ROWS + c, pack, stride=TOKEN_ROWS), :]
                    xg_ref[pl.ds(t0, pack), c * LANES:(c + 1) * LANES] = v.astype(BF16)
                return carry
            lax.fori_loop(0, m // pack, unpack, 0)
            acc_ref[pl.ds(off, m), :] = jnp.zeros((m, acc_ref.shape[1]), F32)

    nxt = jnp.minimum(e + 1, n_e - 1)
    other = 1 - slot
    copies = []
    for s, (_, padded, off) in enumerate(layout):
        chunk = padded // FF_STEPS
        for r in range(chunk):
            copies.append((s, nxt * padded + f * chunk + r, off + f * chunk, r))
    tiles = [(off + i * min(512, m), min(512, m)) for m, _, off in layout for i in range(m // min(512, m))]
    share = -(-len(copies) // len(tiles))

    wg = wg_ref[...].astype(BF16)
    wu = wu_ref[...].astype(BF16)
    wd = wd_ref[...].astype(BF16)
    for ti, (r0, mt) in enumerate(tiles):
        for s, src, dst0, r in copies[ti * share:(ti + 1) * share]:
            tok = idx_refs[s][src]
            dst = pl.multiple_of(dst0 * TOKEN_ROWS, SUBLANES * TOKEN_ROWS) + r * TOKEN_ROWS
            pltpu.make_async_copy(
                tabs[s].at[pl.ds(pl.multiple_of(tok * TOKEN_ROWS, TOKEN_ROWS), TOKEN_ROWS), :],
                xg32_ref.at[other, pl.ds(pl.multiple_of(dst, TOKEN_ROWS), TOKEN_ROWS), :],
                sem.at[other]).start()
        rows = pl.ds(r0, mt)
        xg = xg_ref[rows, :]
        g = jnp.dot(xg, wg, preferred_element_type=F32)
        u = jnp.dot(xg, wu, preferred_element_type=F32)
        hid = (g * jax.nn.sigmoid(g) * u).astype(BF16)
        acc_ref[rows, :] += jnp.dot(hid, wd, preferred_element_type=F32)

    @pl.when((e == n_e - 1) & (f == FF_STEPS - 1))
    def _drain():
        for s, (_, padded, off) in enumerate(layout):
            pltpu.make_async_copy(tabs[s].at[pl.ds(0, padded * TOKEN_ROWS), :],
                                  xg32_ref.at[other, pl.ds(off * TOKEN_ROWS, padded * TOKEN_ROWS), :],
                                  sem.at[other]).wait()

    @pl.when(f == FF_STEPS - 1)
    def _finish():
        end = 0
        for m, _, off in layout:
            if off > end:
                o_ref[pl.ds(end, off - end), :] = jnp.zeros((off - end, o_ref.shape[1]), o_ref.dtype)
            o_ref[pl.ds(off, m), :] = (acc_ref[pl.ds(off, m), :] * gate_ref[pl.ds(off, m), :]).astype(o_ref.dtype)
            end = off + m
        if o_ref.shape[0] > end:
            o_ref[pl.ds(end, o_ref.shape[0] - end), :] = jnp.zeros((o_ref.shape[0] - end, o_ref.shape[1]), o_ref.dtype)


def _ffn_call(layer, row_counts, idxs, gate, tables, w_gate, w_up, w_down):
    ns = len(tables)
    n_e = w_gate.shape[1]
    d = w_gate.shape[2]
    layout, out_rows, scratch_rows = _ffn_layout(row_counts)
    assert gate.shape == (n_e, out_rows, 1)
    grid_spec = pltpu.PrefetchScalarGridSpec(
        num_scalar_prefetch=ns,
        grid=(n_e, FF_STEPS),
        in_specs=[pl.BlockSpec((None, out_rows, 1), lambda e, f, *_: (e, 0, 0))]
        + [pl.BlockSpec(memory_space=pl.ANY)] * ns
        + [pl.BlockSpec((None, None, d, FF_TILE), lambda e, f, *_: (layer, e, 0, f)),
           pl.BlockSpec((None, None, d, FF_TILE), lambda e, f, *_: (layer, e, 0, f)),
           pl.BlockSpec((None, None, FF_TILE, d), lambda e, f, *_: (layer, e, f, 0))],
        out_specs=pl.BlockSpec((None, out_rows, d), lambda e, f, *_: (e, 0, 0)),
        scratch_shapes=[pltpu.VMEM((2, scratch_rows * TOKEN_ROWS, LANES), F32), pltpu.VMEM((scratch_rows, d), BF16),
                        pltpu.VMEM((scratch_rows, d), F32), pltpu.SemaphoreType.DMA((2,))],
    )
    return pl.pallas_call(
        functools.partial(_ffn_kernel, layout),
        grid_spec=grid_spec,
        out_shape=jax.ShapeDtypeStruct((n_e, out_rows, d), BF16),
        compiler_params=_cparams(("arbitrary", "arbitrary")),
        name="ffn",
    )(*idxs, gate, *tables, w_gate, w_up, w_down)


def _combine_kernel(final, *refs):
    if final:
        x_ref, kt_ref, y_ref, g2_ref, fg_ref, o_ref = refs
    else:
        x_ref, kt_ref, y_ref, g2_ref, o_ref = refs
    cap = y_ref.shape[1]
    kt = kt_ref[...]
    lane = lax.broadcasted_iota(jnp.int32, (1, cap), 1).astype(F32)
    acc = jnp.zeros(x_ref.shape, F32)
    for e in range(N_EXPERTS):
        q = (kt[:, e:e + 1] == lane).astype(BF16)
        acc = acc + jnp.dot(q, y_ref[e], preferred_element_type=F32)
    x = x_ref[...] + g2_ref[...] * acc
    if final:
        x = x * lax.rsqrt(jnp.mean(x * x, axis=-1, keepdims=True) + EPS) * fg_ref[...]
    o_ref[...] = x


def _combine_call(x, keyt, yexp, first_row, cap, g2, final_g):
    b, n, d = x.shape
    tc = min(512, n)
    final = final_g is not None
    blk0 = first_row // cap
    row = pl.BlockSpec((None, tc, d), lambda i, j: (i, j, 0))
    in_specs = [row,
                pl.BlockSpec((None, tc, LANES), lambda i, j: (i, j, 0)),
                pl.BlockSpec((N_EXPERTS, cap, d), lambda i, j: (0, blk0 + i, 0)),
                pl.BlockSpec((None, 1, d), lambda i, j: (i, 0, 0))]
    args = [x, keyt, yexp, g2]
    if final:
        in_specs.append(pl.BlockSpec((1, d), lambda i, j: (0, 0)))
        args.append(final_g.reshape(1, d))
    return pl.pallas_call(
        functools.partial(_combine_kernel, final),
        grid=(b, n // tc),
        in_specs=in_specs,
        out_specs=row,
        out_shape=jax.ShapeDtypeStruct((b, n, d), F32),
        compiler_params=_cparams(("arbitrary", "arbitrary")),
        name="combine",
    )(*args)


def _grid_pos_embed(rows, d):
    rr, cc = np.meshgrid(np.arange(rows, dtype=np.float64), np.arange(GRID_W, dtype=np.float64), indexing="ij")
    quarter = d // 4
    omega = 1.0 / (POS_BASE ** (np.arange(quarter, dtype=np.float64) / quarter))

    def enc(p):
        ang = p.reshape(-1)[:, None] * omega
        return np.concatenate([np.sin(ang), np.cos(ang)], axis=-1)

    return np.concatenate([enc(rr), enc(cc)], axis=-1).astype(np.float32)


def _moe(layer, streams, w_gate, w_up, w_down):
    d = streams[0]["x"].shape[-1]
    caps = [EC_CAPACITY * s["x"].shape[1] // N_EXPERTS for s in streams]
    row_counts = [s["x"].shape[0] * cap for s, cap in zip(streams, caps)]
    layout, out_rows, _ = _ffn_layout(row_counts)
    idxs, keyts = [], []
    gate_all = jnp.zeros((N_EXPERTS, out_rows, 1), F32)
    for s, cap, (m, padded, off) in zip(streams, caps, layout):
        b = s["x"].shape[0]
        key, csum, aff, keyt = _route_call(s["logits_t"], cap)
        idx, gate = _compact_call(key, csum, aff, cap)
        idx = idx.reshape(b, N_EXPERTS, cap).transpose(1, 0, 2).reshape(N_EXPERTS, m)
        idxs.append(jnp.pad(idx, ((0, 0), (0, padded - m))).reshape(N_EXPERTS * padded))
        gate = gate.reshape(b, N_EXPERTS, cap).transpose(1, 0, 2).reshape(N_EXPERTS, m, 1)
        gate_all = lax.dynamic_update_slice(gate_all, gate, (0, off, 0))
        keyts.append(keyt)
    tables = [s["h2"] for s in streams]
    yexp = _ffn_call(layer, row_counts, idxs, gate_all, tables, w_gate, w_up, w_down)
    return [_combine_call(s["x"], keyt, yexp, off, cap, s["g2"], s["final_g"])
            for s, keyt, cap, (_, _, off) in zip(streams, keyts, caps, layout)]


def kernel(x, c, ctx, c_ctx, w_mod, b_mod, norm1_g, norm2_g, w_in, conv_w, conv_b, lru_wa, lru_ba,
           lru_wi, lru_bi, lru_lambda, w_out, w_router, w_gate, w_up, w_down, final_g):
    bsz, n, d = x.shape
    depth = w_mod.shape[0]
    pos = jnp.asarray(_grid_pos_embed(n // GRID_W, d))

    crows = jnp.zeros((SUBLANES, d), F32).at[:bsz].set(c).at[bsz].set(c_ctx)
    mod = _mod_call(crows, w_mod, b_mod)

    for l in range(depth):
        last = l == depth - 1
        mx = mod[l, :bsz].reshape(bsz, 1, N_MOD, d)
        sh1, sc1, g1, sh2, sc2, g2 = [mx[:, :, i, :] for i in range(N_MOD)]
        mc = jnp.broadcast_to(mod[l, bsz].reshape(1, 1, N_MOD, d), (bsz, 1, N_MOD, d))
        csh1, csc1, cg1, csh2, csc2, cg2 = [mc[:, :, i, :] for i in range(N_MOD)]
        w_in_bf = w_in[l]
        w_out_bf = w_out[l]
        w_router_t = w_router[l].T
        lp = _lru_params(conv_w[l], conv_b[l], lru_wa[l], lru_ba[l], lru_wi[l], lru_bi[l], lru_lambda[l])

        cuf, cux, cug = _proj_call(ctx, None, norm1_g[l], csh1, csc1, w_in_bf)
        cyr, ctx_states = _lru_call(cux, cug, lp, None)
        streams = []
        if not last:
            cyf = _fourier_call(cuf)
            ctx, ch2, clg = _mixout_call(ctx, cyf, cyr, w_out_bf, cg1, norm2_g[l], csh2, csc2, w_router_t)
            streams.append(dict(x=ctx, h2=ch2, logits_t=clg, g2=cg2, final_g=None))

        if l == 0:
            uf, ux, ug, x = _proj_call(x, pos, norm1_g[l], sh1, sc1, w_in_bf)
        else:
            uf, ux, ug = _proj_call(x, None, norm1_g[l], sh1, sc1, w_in_bf)
        yf = _fourier_call(uf)
        yr, _ = _lru_call(ux, ug, lp, ctx_states)
        x, h2, lg = _mixout_call(x, yf, yr, w_out_bf, g1, norm2_g[l], sh2, sc2, w_router_t)
        streams.insert(0, dict(x=x, h2=h2, logits_t=lg, g2=g2, final_g=final_g if last else None))
        outs = _moe(l, streams, w_gate, w_up, w_down)
        x = outs[0]
        if not last:
            ctx = outs[1]
    return x
```

```python
import functools
import math

import numpy as np
import jax
import jax.numpy as jnp
from jax import lax
from jax.experimental import pallas as pl
from jax.experimental.pallas import tpu as pltpu

D_MODEL = 1024
GRID_W = 64
POS_BASE = 10000.0
D_FOURIER = 512
FOURIER_GROUPS = 4
D_GROUP = D_FOURIER // FOURIER_GROUPS
D_LRU = 512
LRU_HEADS = 8
LRU_HEAD_DIM = D_LRU // LRU_HEADS
CONV_W = 4
LRU_C = 8.0
D_IN = D_FOURIER + 2 * D_LRU
N_EXPERTS = 16
EC_CAPACITY = 2
D_FF = 2816
N_MOD = 6
EPS = 1e-6

LANES = 128
SUBLANES = 8
DFT_BLOCK = 256
TOKEN_ROWS = D_MODEL // LANES
VMEM_LIMIT = 56 * 1024 * 1024

F32 = jnp.float32
BF16 = jnp.bfloat16


def _cparams(sem, **kw):
    return pltpu.CompilerParams(dimension_semantics=sem, vmem_limit_bytes=VMEM_LIMIT, **kw)


def _mod_kernel(c_ref, w_ref, b_ref, o_ref):
    c = c_ref[...]
    cond = c * jax.nn.sigmoid(c)
    o_ref[...] = jnp.dot(cond, w_ref[...], preferred_element_type=F32,
                         precision=lax.Precision.HIGHEST) + b_ref[...]


def _mod_call(crows, w_mod, b_mod):
    depth, d, nm = w_mod.shape
    tn = 1536
    return pl.pallas_call(
        _mod_kernel,
        grid=(depth, nm // tn),
        in_specs=[
            pl.BlockSpec((SUBLANES, d), lambda l, j: (0, 0)),
            pl.BlockSpec((None, d, tn), lambda l, j: (l, 0, j)),
            pl.BlockSpec((None, 1, tn), lambda l, j: (l, 0, j)),
        ],
        out_specs=pl.BlockSpec((None, SUBLANES, tn), lambda l, j: (l, 0, j)),
        out_shape=jax.ShapeDtypeStruct((depth, SUBLANES, nm), F32),
        compiler_params=_cparams(("arbitrary", "arbitrary")),
        name="mod",
    )(crows, w_mod, b_mod.reshape(depth, 1, nm))


def _rms_mod(x, g, shift, scale):
    y = x * lax.rsqrt(jnp.mean(x * x, axis=-1, keepdims=True) + EPS) * g
    return y * (1.0 + scale) + shift


def _proj_kernel(add_pos, *refs):
    if add_pos:
        x_ref, pos_ref, g_ref, sh_ref, sc_ref, w_ref, uf_ref, ux_ref, ug_ref, xp_ref, wbf_ref = refs
        x = x_ref[...] + pos_ref[...]
        xp_ref[...] = x
    else:
        x_ref, g_ref, sh_ref, sc_ref, w_ref, uf_ref, ux_ref, ug_ref, wbf_ref = refs
        x = x_ref[...]

    @pl.when((pl.program_id(0) == 0) & (pl.program_id(1) == 0))
    def _cast_w():
        wbf_ref[...] = w_ref[...].astype(BF16)

    h = _rms_mod(x, g_ref[...], sh_ref[...], sc_ref[...])
    u = jnp.dot(h.astype(BF16), wbf_ref[...], preferred_element_type=F32)
    uf_ref[...] = u[:, :D_FOURIER]
    ux_ref[...] = u[:, D_FOURIER:D_FOURIER + D_LRU]
    ug_ref[...] = u[:, D_FOURIER + D_LRU:]


def _proj_call(x, pos, g, shift, scale, w_in_bf):
    b, n, d = x.shape
    tm = min(512, n)
    add_pos = pos is not None
    row = pl.BlockSpec((None, tm, d), lambda i, j: (i, j, 0))
    vec_b = pl.BlockSpec((None, 1, d), lambda i, j: (i, 0, 0))
    in_specs = [row]
    args = [x]
    if add_pos:
        in_specs.append(pl.BlockSpec((tm, d), lambda i, j: (j, 0)))
        args.append(pos)
    in_specs += [pl.BlockSpec((1, d), lambda i, j: (0, 0)), vec_b, vec_b,
                 pl.BlockSpec((d, D_IN), lambda i, j: (0, 0))]
    args += [g.reshape(1, d), shift, scale, w_in_bf]
    part = pl.BlockSpec((None, tm, D_FOURIER), lambda i, j: (i, j, 0))
    out_specs = [part, part, part]
    out_shape = [jax.ShapeDtypeStruct((b, n, D_FOURIER), F32)] * 3
    if add_pos:
        out_specs.append(row)
        out_shape.append(jax.ShapeDtypeStruct((b, n, d), F32))
    return pl.pallas_call(
        functools.partial(_proj_kernel, add_pos),
        grid=(b, n // tm),
        in_specs=in_specs,
        out_specs=out_specs,
        out_shape=out_shape,
        scratch_shapes=[pltpu.VMEM((d, D_IN), BF16)],
        compiler_params=_cparams(("arbitrary", "arbitrary")),
        name="proj",
    )(*args)


def _cmul_const(re, im, c, s):
    tol = 1e-12
    if abs(s) < tol:
        if abs(c - 1.0) < tol:
            return re, im
        if abs(c + 1.0) < tol:
            return -re, -im
        return re * c, im * c
    if abs(c) < tol:
        if abs(s - 1.0) < tol:
            return -im, re
        if abs(s + 1.0) < tol:
            return im, -re
        return -im * s, re * s
    return re * c - im * s, re * s + im * c


def _fft_list(xs):
    n = len(xs)
    if n == 1:
        return xs
    ev = _fft_list(xs[0::2])
    od = _fft_list(xs[1::2])
    out = [None] * n
    for k in range(n // 2):
        ang = -2.0 * math.pi * k / n
        tr, ti = _cmul_const(od[k][0], od[k][1], math.cos(ang), math.sin(ang))
        out[k] = (ev[k][0] + tr, ev[k][1] + ti)
        out[k + n // 2] = (ev[k][0] - tr, ev[k][1] - ti)
    return out


def _fourier_kernel(n1_count, u_ref, cs_ref, twc_ref, tws_ref, cc_ref, o_ref, tr_ref, ti_ref, pq_ref):
    n = u_ref.shape[0]
    cs = cs_ref[...].astype(BF16)
    for n1 in range(n1_count):
        if n1_count == 1:
            z = u_ref[...]
        else:
            z = u_ref[pl.ds(n1, DFT_BLOCK, stride=n1_count), :]
        g = jnp.dot(cs, z.astype(BF16), preferred_element_type=F32)
        gr = g[:DFT_BLOCK]
        gs = g[DFT_BLOCK:]
        if n1 == 0:
            tr_ref[n1] = gr
            ti_ref[n1] = -gs
        else:
            c = twc_ref[n1]
            s = tws_ref[n1]
            tr_ref[n1] = gr * c - gs * s
            ti_ref[n1] = -(gr * s + gs * c)

    if n1_count == 1:
        pq_ref[:, :D_GROUP] = tr_ref[0]
        pq_ref[:, D_GROUP:] = -ti_ref[0]
    else:
        def chunk(j, carry):
            r0 = pl.multiple_of(j * SUBLANES, SUBLANES)
            xs = [(tr_ref[i, pl.ds(r0, SUBLANES), :], ti_ref[i, pl.ds(r0, SUBLANES), :])
                  for i in range(n1_count)]
            ys = _fft_list(xs)
            for k1 in range(n1_count):
                rows = pl.ds(pl.multiple_of(k1 * DFT_BLOCK + r0, SUBLANES), SUBLANES)
                pq_ref[rows, :D_GROUP] = ys[k1][0]
                pq_ref[rows, D_GROUP:] = -ys[k1][1]
            return carry
        lax.fori_loop(0, DFT_BLOCK // SUBLANES, chunk, 0)

    cc = cc_ref[...].astype(BF16)
    tile = min(512, n)
    for i in range(n // tile):
        rows = pl.ds(i * tile, tile)
        o_ref[rows, :] = jnp.dot(pq_ref[rows, :].astype(BF16), cc, preferred_element_type=F32)


@functools.lru_cache(maxsize=None)
def _fourier_consts(n):
    n1c = n // DFT_BLOCK
    k = np.arange(DFT_BLOCK, dtype=np.float64)
    ang = 2.0 * np.pi * np.outer(k, k) / DFT_BLOCK
    cs = np.concatenate([np.cos(ang), np.sin(ang)], axis=0)
    tw = 2.0 * np.pi * np.outer(np.arange(n1c, dtype=np.float64), k) / n
    twc = np.repeat(np.cos(tw)[:, :, None], LANES, axis=2)
    tws = np.repeat(np.sin(tw)[:, :, None], LANES, axis=2)
    c = np.arange(D_GROUP, dtype=np.float64)
    angc = 2.0 * np.pi * np.outer(c, c) / D_GROUP
    scale = 1.0 / math.sqrt(n * D_GROUP)
    cc = np.concatenate([np.cos(angc), -np.sin(angc)], axis=0) * scale
    return (cs.astype(np.float32), twc.astype(np.float32), tws.astype(np.float32), cc.astype(np.float32))


def _fourier_call(u):
    b, n, _ = u.shape
    n1c = n // DFT_BLOCK
    cs, twc, tws, cc = _fourier_consts(n)
    blk = pl.BlockSpec((None, n, D_GROUP), lambda i, j: (i, 0, j))
    return pl.pallas_call(
        functools.partial(_fourier_kernel, n1c),
        grid=(b, FOURIER_GROUPS),
        in_specs=[
            blk,
            pl.BlockSpec((2 * DFT_BLOCK, DFT_BLOCK), lambda i, j: (0, 0)),
            pl.BlockSpec((n1c, DFT_BLOCK, LANES), lambda i, j: (0, 0, 0)),
            pl.BlockSpec((n1c, DFT_BLOCK, LANES), lambda i, j: (0, 0, 0)),
            pl.BlockSpec((2 * D_GROUP, D_GROUP), lambda i, j: (0, 0)),
        ],
        out_specs=blk,
        out_shape=jax.ShapeDtypeStruct(u.shape, F32),
        scratch_shapes=[
            pltpu.VMEM((n1c, DFT_BLOCK, D_GROUP), F32),
            pltpu.VMEM((n1c, DFT_BLOCK, D_GROUP), F32),
            pltpu.VMEM((n, 2 * D_GROUP), F32),
        ],
        compiler_params=_cparams(("arbitrary", "arbitrary")),
        name="fourier",
    )(u, jnp.asarray(cs), jnp.asarray(twc), jnp.asarray(tws), jnp.asarray(cc))


SEGMENTS = SUBLANES
PAD_ROWS = 8


def _expm1(x):
    u = jnp.exp(x)
    near = (u - 1.0) * x / jnp.log(jnp.where(u == 1.0, 2.0, u))
    return jnp.where(x < -0.5, u - 1.0, jnp.where(u == 1.0, x, near))


def _lru_kernel(has_h0, *refs):
    if has_h0:
        (ux_ref, ug_ref, cw_ref, cb_ref, wg_ref, bg_ref, lam_ref, h0_ref,
         y_ref, fin_ref, upad_ref, a_ref, b_ref, h_ref) = refs
    else:
        (ux_ref, ug_ref, cw_ref, cb_ref, wg_ref, bg_ref, lam_ref,
         y_ref, fin_ref, upad_ref, a_ref, b_ref, h_ref) = refs
    n = ux_ref.shape[0]
    seg_len = n // SEGMENTS
    tile = min(256, seg_len)

    zeros_pad = jnp.zeros((PAD_ROWS, LANES), F32)
    upad_ref[pl.ds(0, PAD_ROWS), :] = zeros_pad
    upad_ref[pl.ds(PAD_ROWS + n, PAD_ROWS), :] = zeros_pad
    upad_ref[pl.ds(PAD_ROWS, n), :] = ux_ref[...]

    lam = lam_ref[...]
    sp = jnp.maximum(-lam, 0.0) + jnp.log1p(jnp.exp(-jnp.abs(lam)))
    cw = cw_ref[...]
    cb = cb_ref[...]
    wg = wg_ref[...].astype(BF16)
    bg = bg_ref[...]
    left = (CONV_W - 1) // 2

    for t in range(n // tile):
        r0 = t * tile
        xc = cb
        for k in range(CONV_W):
            xc = xc + upad_ref[pl.ds(PAD_ROWS + r0 + k - left, tile), :] * cw[k:k + 1, :]
        gates = jnp.dot(xc.astype(BF16), wg, preferred_element_type=F32) + bg
        seg = r0 // seg_len
        s0 = r0 - seg * seg_len
        for d in range(2):
            r = jax.nn.sigmoid(gates[:, (2 * d) * LANES:(2 * d + 1) * LANES])
            gi = jax.nn.sigmoid(gates[:, (2 * d + 1) * LANES:(2 * d + 2) * LANES])
            log_a = (-LRU_C) * r * sp[d:d + 1, :]
            a = jnp.exp(log_a)
            drive = jnp.sqrt(-_expm1(2.0 * log_a)) * (gi * xc)
            dst = pl.ds(s0 * SEGMENTS + seg, tile, stride=SEGMENTS)
            a_ref[d, dst, :] = a
            b_ref[d, dst, :] = drive

    def rows(s):
        return pl.ds(pl.multiple_of(s * SEGMENTS, SEGMENTS), SEGMENTS)

    def pass1(t, carry):
        hf, af, hb, ab = carry
        sb = seg_len - 1 - t
        a0 = a_ref[0, rows(t), :]
        a1 = a_ref[1, rows(sb), :]
        hf = a0 * hf + b_ref[0, rows(t), :]
        hb = a1 * hb + b_ref[1, rows(sb), :]
        return hf, af * a0, hb, ab * a1

    z = jnp.zeros((SEGMENTS, LANES), F32)
    o = jnp.ones((SEGMENTS, LANES), F32)
    hf, af, hb, ab = lax.fori_loop(0, seg_len, pass1, (z, o, z, o), unroll=8)

    if has_h0:
        cf = h0_ref[0:1, :]
        cbk = h0_ref[1:2, :]
    else:
        cf = jnp.zeros((1, LANES), F32)
        cbk = jnp.zeros((1, LANES), F32)
    sub = lax.broadcasted_iota(jnp.int32, (SEGMENTS, LANES), 0)
    carry_f = z
    for j in range(SEGMENTS):
        carry_f = jnp.where(sub == j, cf, carry_f)
        cf = hf[j:j + 1, :] + af[j:j + 1, :] * cf
    carry_b = z
    for j in reversed(range(SEGMENTS)):
        carry_b = jnp.where(sub == j, cbk, carry_b)
        cbk = hb[j:j + 1, :] + ab[j:j + 1, :] * cbk
    fin_ref[0:1, :] = cf
    fin_ref[1:2, :] = cbk

    def pass2(t, carry):
        hf, hb = carry
        sb = seg_len - 1 - t
        hf = a_ref[0, rows(t), :] * hf + b_ref[0, rows(t), :]
        h_ref[0, rows(t), :] = hf
        hb = a_ref[1, rows(sb), :] * hb + b_ref[1, rows(sb), :]
        h_ref[1, rows(sb), :] = hb
        return hf, hb

    lax.fori_loop(0, seg_len, pass2, (carry_f, carry_b), unroll=8)

    for t in range(n // tile):
        r0 = t * tile
        seg = r0 // seg_len
        s0 = r0 - seg * seg_len
        src = pl.ds(s0 * SEGMENTS + seg, tile, stride=SEGMENTS)
        hsum = h_ref[0, src, :] + h_ref[1, src, :]
        y_ref[pl.ds(r0, tile), :] = hsum * jax.nn.gelu(ug_ref[pl.ds(r0, tile), :])


def _lru_call(ux, ug, lp, h0):
    b, n, _ = ux.shape
    nblk = D_LRU // LANES
    has_h0 = h0 is not None
    blk = pl.BlockSpec((None, n, LANES), lambda i, j: (i, 0, j))
    in_specs = [
        blk, blk,
        pl.BlockSpec((CONV_W, LANES), lambda i, j: (0, j)),
        pl.BlockSpec((1, LANES), lambda i, j: (0, j)),
        pl.BlockSpec((None, LANES, 4 * LANES), lambda i, j: (j, 0, 0)),
        pl.BlockSpec((None, 1, 4 * LANES), lambda i, j: (j, 0, 0)),
        pl.BlockSpec((2, LANES), lambda i, j: (0, j)),
    ]
    args = [ux, ug, lp["conv_w"], lp["conv_b"], lp["w_gates"], lp["b_gates"], lp["lam"]]
    if has_h0:
        in_specs.append(pl.BlockSpec((None, 2, LANES), lambda i, j: (i, 0, j)))
        args.append(h0)
    return pl.pallas_call(
        functools.partial(_lru_kernel, has_h0),
        grid=(b, nblk),
        in_specs=in_specs,
        out_specs=[blk, pl.BlockSpec((None, 2, LANES), lambda i, j: (i, 0, j))],
        out_shape=[jax.ShapeDtypeStruct((b, n, D_LRU), F32),
                   jax.ShapeDtypeStruct((b, 2, D_LRU), F32)],
        scratch_shapes=[
            pltpu.VMEM((n + 2 * PAD_ROWS, LANES), F32),
            pltpu.VMEM((2, n, LANES), F32),
            pltpu.VMEM((2, n, LANES), F32),
            pltpu.VMEM((2, n, LANES), F32),
        ],
        compiler_params=_cparams(("arbitrary", "arbitrary")),
        name="lru",
    )(*args)


def _lru_params(conv_w, conv_b, wa, ba, wi, bi, lam):
    nblk = D_LRU // LANES
    hpb = LANES // LRU_HEAD_DIM

    def blockdiag(w):
        w = w.reshape(nblk, hpb, LRU_HEAD_DIM, LRU_HEAD_DIM)
        eye = jnp.eye(hpb, dtype=w.dtype)
        return jnp.einsum("bhij,hg->bhigj", w, eye).reshape(nblk, LANES, LANES)

    w_gates = jnp.concatenate([blockdiag(wa[0]), blockdiag(wi[0]), blockdiag(wa[1]), blockdiag(wi[1])], axis=-1)
    b_gates = jnp.stack([ba[0], bi[0], ba[1], bi[1]], axis=0)
    b_gates = b_gates.reshape(4, nblk, LANES).transpose(1, 0, 2).reshape(nblk, 1, 4 * LANES)
    return {"conv_w": conv_w, "conv_b": conv_b.reshape(1, D_LRU), "w_gates": w_gates,
            "b_gates": b_gates, "lam": lam}


def _mixout_kernel(x_ref, yf_ref, yr_ref, w_ref, g1_ref, n2_ref, sh_ref, sc_ref, wr_ref,
                   xo_ref, h2_ref, lg_ref, wbf_ref):
    @pl.when((pl.program_id(0) == 0) & (pl.program_id(1) == 0))
    def _cast_w():
        wbf_ref[...] = w_ref[...].astype(BF16)

    mix = jnp.dot(yf_ref[...].astype(BF16), wbf_ref[:D_FOURIER, :], preferred_element_type=F32)
    mix = mix + jnp.dot(yr_ref[...].astype(BF16), wbf_ref[D_FOURIER:, :], preferred_element_type=F32)
    x = x_ref[...] + g1_ref[...] * mix
    xo_ref[...] = x
    h2 = _rms_mod(x, n2_ref[...], sh_ref[...], sc_ref[...])
    tm = h2.shape[0]
    for s in range(TOKEN_ROWS):
        h2_ref[pl.ds(s, tm, stride=TOKEN_ROWS), :] = h2[:, s * LANES:(s + 1) * LANES]
    lg_ref[...] = lax.dot_general(wr_ref[...], h2, (((1,), (1,)), ((), ())),
                                  preferred_element_type=F32, precision=lax.Precision.HIGHEST)


def _mixout_call(x, yf, yr, w_out_bf, g1, n2g, sh2, sc2, w_router_t):
    b, n, d = x.shape
    tm = min(512, n)
    row = pl.BlockSpec((None, tm, d), lambda i, j: (i, j, 0))
    half = pl.BlockSpec((None, tm, D_FOURIER), lambda i, j: (i, j, 0))
    vec_b = pl.BlockSpec((None, 1, d), lambda i, j: (i, 0, 0))
    return pl.pallas_call(
        _mixout_kernel,
        grid=(b, n // tm),
        in_specs=[row, half, half,
                  pl.BlockSpec((d, d), lambda i, j: (0, 0)),
                  vec_b,
                  pl.BlockSpec((1, d), lambda i, j: (0, 0)),
                  vec_b, vec_b,
                  pl.BlockSpec((N_EXPERTS, d), lambda i, j: (0, 0))],
        out_specs=[row,
                   pl.BlockSpec((tm * TOKEN_ROWS, LANES), lambda i, j: (i * (n // tm) + j, 0)),
                   pl.BlockSpec((None, N_EXPERTS, tm), lambda i, j: (i, 0, j))],
        out_shape=[jax.ShapeDtypeStruct((b, n, d), F32),
                   jax.ShapeDtypeStruct((b * n * TOKEN_ROWS, LANES), F32),
                   jax.ShapeDtypeStruct((b, N_EXPERTS, n), F32)],
        scratch_shapes=[pltpu.VMEM((d, d), BF16)],
        compiler_params=_cparams(("arbitrary", "arbitrary")),
        name="mixout",
    )(x, yf, yr, w_out_bf, g1, n2g.reshape(1, d), sh2, sc2, w_router_t)


def _lane_prefix_sum(x):
    n = x.shape[-1]
    lane = lax.broadcasted_iota(jnp.int32, x.shape, x.ndim - 1)
    s = 1
    while s < n:
        x = x + jnp.where(lane >= s, pltpu.roll(x, s, x.ndim - 1), 0)
        s *= 2
    return x


def _route_kernel(cap, lg_ref, csum_ref, keyt_ref, afft_ref):
    n = lg_ref.shape[1]
    lg = lg_ref[...]
    m = jnp.max(lg, axis=0, keepdims=True)
    ex = jnp.exp(lg - m)
    aff = ex / jnp.sum(ex, axis=0, keepdims=True)
    bits = pltpu.bitcast(aff, jnp.int32)

    def bis(_, carry):
        lo, hi = carry
        mid = lo + ((hi - lo + 1) >> 1)
        cnt = jnp.sum((bits >= mid).astype(jnp.int32), axis=1, keepdims=True)
        ok = cnt >= cap
        return jnp.where(ok, mid, lo), jnp.where(ok, hi, mid - 1)

    lo0 = jnp.zeros((N_EXPERTS, 1), jnp.int32)
    hi0 = jnp.full((N_EXPERTS, 1), 0x7F800000, jnp.int32)
    thr, _ = lax.fori_loop(0, 32, bis, (lo0, hi0))

    thr_f = pltpu.bitcast(thr, F32)
    window = 2.0 ** -6
    flo0 = thr_f * (1.0 - window)
    fhi0 = jnp.maximum(thr_f * (1.0 + window), jnp.float32(1e-37))

    def fbis(_, carry):
        lo, hi = carry
        mid = lo + 0.5 * (hi - lo)
        cnt = jnp.sum((aff >= mid).astype(jnp.int32), axis=1, keepdims=True)
        ok = cnt >= cap
        return jnp.where(ok, mid, lo), jnp.where(ok, hi, mid)

    flo, fhi = lax.fori_loop(0, 32, fbis, (flo0, fhi0))
    gt = aff >= fhi
    eq = (aff >= flo) & (aff < fhi)
    need = cap - jnp.sum(gt.astype(jnp.int32), axis=1, keepdims=True)
    eq_rank = _lane_prefix_sum(eq.astype(jnp.int32)) - eq.astype(jnp.int32)
    sel = gt | (eq & (eq_rank < need))
    csum = _lane_prefix_sum(sel.astype(jnp.int32))
    key = jnp.where(sel, csum - 1, -1)
    csum_ref[...] = csum

    pad = jnp.zeros((LANES - N_EXPERTS, n), F32)
    keyt_ref[...] = jnp.concatenate([key.astype(F32), pad - 1.0], axis=0).T
    afft_ref[...] = jnp.concatenate([aff, pad], axis=0).T


def _route_call(logits_t, cap):
    b, _, n = logits_t.shape
    en = pl.BlockSpec((None, N_EXPERTS, n), lambda i: (i, 0, 0))
    tm = pl.BlockSpec((None, n, LANES), lambda i: (i, 0, 0))
    return pl.pallas_call(
        functools.partial(_route_kernel, cap),
        grid=(b,),
        in_specs=[en],
        out_specs=[en, tm, tm],
        out_shape=[jax.ShapeDtypeStruct((b, N_EXPERTS, n), jnp.int32),
                   jax.ShapeDtypeStruct((b, n, LANES), F32),
                   jax.ShapeDtypeStruct((b, n, LANES), F32)],
        compiler_params=_cparams(("arbitrary",)),
        name="route",
    )(logits_t)


def _compact_kernel(cap, klo_ref, khi_ref, keyt_ref, idx_ref):
    b = pl.program_id(0)
    n = keyt_ref.shape[0]
    rt = min(LANES, cap)
    ntile = cap // rt
    slot = lax.broadcasted_iota(jnp.int32, (1, rt), 1).astype(F32)
    tok0 = (b * n + lax.broadcasted_iota(jnp.int32, (LANES, 1), 0)).astype(F32)
    zero = jnp.zeros((LANES, rt), F32)
    for e in range(N_EXPERTS):
        for t in range(ntile):
            s = (b * N_EXPERTS + e) * ntile + t
            want = slot + float(t * rt)

            def block(k, ids, e=e, want=want):
                r0 = pl.multiple_of(k * LANES, LANES)
                p = keyt_ref[pl.ds(r0, LANES), e:e + 1] == want
                return ids + jnp.where(p, tok0 + (k * LANES).astype(F32), 0.0)

            ids = lax.fori_loop(klo_ref[s], khi_ref[s], block, zero)
            idx_ref[e:e + 1, t * rt:(t + 1) * rt] = jnp.sum(ids, axis=0, keepdims=True).astype(jnp.int32)


def _compact_call(keyt, csum, cap):
    b, n, _ = keyt.shape
    rt = min(LANES, cap)
    ntile = cap // rt
    cb_incl = csum[:, :, LANES - 1::LANES]
    cb_excl = jnp.concatenate([jnp.zeros_like(cb_incl[..., :1]), cb_incl[..., :-1]], axis=-1)
    t0 = (jnp.arange(ntile, dtype=jnp.int32) * rt)[:, None]
    klo = jnp.sum((cb_incl[:, :, None, :] <= t0).astype(jnp.int32), axis=-1)
    khi = jnp.sum((cb_excl[:, :, None, :] < t0 + rt).astype(jnp.int32), axis=-1)
    return pl.pallas_call(
        functools.partial(_compact_kernel, cap),
        grid_spec=pltpu.PrefetchScalarGridSpec(
            num_scalar_prefetch=2, grid=(b,),
            in_specs=[pl.BlockSpec((None, n, LANES), lambda i, *_: (i, 0, 0))],
            out_specs=pl.BlockSpec((None, N_EXPERTS, cap), lambda i, *_: (i, 0, 0))),
        out_shape=jax.ShapeDtypeStruct((b, N_EXPERTS, cap), jnp.int32),
        compiler_params=_cparams(("arbitrary",)),
        name="compact",
    )(klo.reshape(-1), khi.reshape(-1), keyt)


FF_TILE = 256


FF_STEPS = D_FF // FF_TILE
ROW_ALIGN = 32


def _round_up(v, m):
    return -(-v // m) * m


def _ffn_layout(row_counts):
    layout, off = [], 0
    for m in row_counts:
        padded = _round_up(m, FF_STEPS * SUBLANES)
        layout.append((m, padded, off))
        off = _round_up(off + padded, ROW_ALIGN)
    out_rows = _round_up(layout[-1][2] + layout[-1][0], ROW_ALIGN)
    scratch_rows = _round_up(layout[-1][2] + layout[-1][1], SUBLANES)
    return tuple(layout), out_rows, max(scratch_rows, out_rows)


def _ffn_kernel(layout, *refs):
    ns = len(layout)
    idx_refs = refs[:ns]
    tabs = refs[ns:2 * ns]
    wg_ref, wu_ref, wd_ref, o_ref, xg32_ref, xg_ref, acc_ref, sem = refs[2 * ns:]
    e = pl.program_id(0)
    f = pl.program_id(1)
    n_e = pl.num_programs(0)
    slot = e % 2

    def issue_chunk(expert, dst_slot, step):
        for s, (_, padded, off) in enumerate(layout):
            chunk = padded // FF_STEPS
            src0 = expert * padded + step * chunk
            dst0 = off + step * chunk

            def body(g, carry, s=s, src0=src0, dst0=dst0):
                src = src0 + g * SUBLANES
                dst = pl.multiple_of(dst0 + g * SUBLANES, SUBLANES)
                for j in range(SUBLANES):
                    tok = idx_refs[s][src + j]
                    pltpu.make_async_copy(
                        tabs[s].at[pl.ds(pl.multiple_of(tok * TOKEN_ROWS, TOKEN_ROWS), TOKEN_ROWS), :],
                        xg32_ref.at[dst_slot, pl.ds(pl.multiple_of((dst + j) * TOKEN_ROWS, TOKEN_ROWS), TOKEN_ROWS), :],
                        sem.at[dst_slot]).start()
                return carry
            lax.fori_loop(0, chunk // SUBLANES, body, 0)

    @pl.when((e == 0) & (f == 0))
    def _first_expert():
        for step in range(FF_STEPS):
            issue_chunk(0, 0, step)

    @pl.when(f == 0)
    def _start_expert():
        for s, (m, padded, off) in enumerate(layout):
            pltpu.make_async_copy(tabs[s].at[pl.ds(0, padded * TOKEN_ROWS), :],
                                  xg32_ref.at[slot, pl.ds(off * TOKEN_ROWS, padded * TOKEN_ROWS), :],
                                  sem.at[slot]).wait()
        pack = 2 * SUBLANES
        for m, _, off in layout:
            def unpack(g, carry, off=off):
                t0 = pl.multiple_of(off + g * pack, pack)
                for c in range(TOKEN_ROWS):
                    v = xg32_ref[slot, pl.ds(t0 * TOKEN_ROWS + c, pack, stride=TOKEN_ROWS), :]
                    xg_ref[pl.ds(t0, pack), c * LANES:(c + 1) * LANES] = v.astype(BF16)
                return carry
            lax.fori_loop(0, m // pack, unpack, 0)
            acc_ref[pl.ds(off, m), :] = jnp.zeros((m, acc_ref.shape[1]), F32)

    nxt = jnp.minimum(e + 1, n_e - 1)
    other = 1 - slot
    copies = []
    for s, (_, padded, off) in enumerate(layout):
        chunk = padded // FF_STEPS
        for r in range(chunk):
            copies.append((s, nxt * padded + f * chunk + r, off + f * chunk, r))
    tiles = [(off + i * min(512, m), min(512, m)) for m, _, off in layout for i in range(m // min(512, m))]
    share = -(-len(copies) // len(tiles))

    wg = wg_ref[...].astype(BF16)
    wu = wu_ref[...].astype(BF16)
    wd = wd_ref[...].astype(BF16)
    for ti, (r0, mt) in enumerate(tiles):
        for s, src, dst0, r in copies[ti * share:(ti + 1) * share]:
            tok = idx_refs[s][src]
            dst = pl.multiple_of(dst0 * TOKEN_ROWS, SUBLANES * TOKEN_ROWS) + r * TOKEN_ROWS
            pltpu.make_async_copy(
                tabs[s].at[pl.ds(pl.multiple_of(tok * TOKEN_ROWS, TOKEN_ROWS), TOKEN_ROWS), :],
                xg32_ref.at[other, pl.ds(pl.multiple_of(dst, TOKEN_ROWS), TOKEN_ROWS), :],
                sem.at[other]).start()
        rows = pl.ds(r0, mt)
        xg = xg_ref[rows, :]
        g = jnp.dot(xg, wg, preferred_element_type=F32)
        u = jnp.dot(xg, wu, preferred_element_type=F32)
        hid = (g * jax.nn.sigmoid(g) * u).astype(BF16)
        acc_ref[rows, :] += jnp.dot(hid, wd, preferred_element_type=F32)

    @pl.when((e == n_e - 1) & (f == FF_STEPS - 1))
    def _drain():
        for s, (_, padded, off) in enumerate(layout):
            pltpu.make_async_copy(tabs[s].at[pl.ds(0, padded * TOKEN_ROWS), :],
                                  xg32_ref.at[other, pl.ds(off * TOKEN_ROWS, padded * TOKEN_ROWS), :],
                                  sem.at[other]).wait()

    @pl.when(f == FF_STEPS - 1)
    def _finish():
        end = 0
        for m, _, off in layout:
            if off > end:
                o_ref[pl.ds(end, off - end), :] = jnp.zeros((off - end, o_ref.shape[1]), o_ref.dtype)
            o_ref[pl.ds(off, m), :] = acc_ref[pl.ds(off, m), :].astype(o_ref.dtype)
            end = off + m
        if o_ref.shape[0] > end:
            o_ref[pl.ds(end, o_ref.shape[0] - end), :] = jnp.zeros((o_ref.shape[0] - end, o_ref.shape[1]), o_ref.dtype)


def _ffn_call(layer, row_counts, idxs, tables, w_gate, w_up, w_down):
    ns = len(tables)
    n_e = w_gate.shape[1]
    d = w_gate.shape[2]
    layout, out_rows, scratch_rows = _ffn_layout(row_counts)
    grid_spec = pltpu.PrefetchScalarGridSpec(
        num_scalar_prefetch=ns,
        grid=(n_e, FF_STEPS),
        in_specs=[pl.BlockSpec(memory_space=pl.ANY)] * ns
        + [pl.BlockSpec((None, None, d, FF_TILE), lambda e, f, *_: (layer, e, 0, f)),
           pl.BlockSpec((None, None, d, FF_TILE), lambda e, f, *_: (layer, e, 0, f)),
           pl.BlockSpec((None, None, FF_TILE, d), lambda e, f, *_: (layer, e, f, 0))],
        out_specs=pl.BlockSpec((None, out_rows, d), lambda e, f, *_: (e, 0, 0)),
        scratch_shapes=[pltpu.VMEM((2, scratch_rows * TOKEN_ROWS, LANES), F32), pltpu.VMEM((scratch_rows, d), BF16),
                        pltpu.VMEM((scratch_rows, d), F32), pltpu.SemaphoreType.DMA((2,))],
    )
    return pl.pallas_call(
        functools.partial(_ffn_kernel, layout),
        grid_spec=grid_spec,
        out_shape=jax.ShapeDtypeStruct((n_e, out_rows, d), BF16),
        compiler_params=_cparams(("arbitrary", "arbitrary")),
        name="ffn",
    )(*idxs, *tables, w_gate, w_up, w_down)


def _combine_kernel(final, *refs):
    if final:
        x_ref, kt_ref, at_ref, y_ref, g2_ref, fg_ref, o_ref = refs
    else:
        x_ref, kt_ref, at_ref, y_ref, g2_ref, o_ref = refs
    cap = y_ref.shape[1]
    kt = kt_ref[...]
    at = at_ref[...]
    lane = lax.broadcasted_iota(jnp.int32, (1, cap), 1).astype(F32)
    acc = jnp.zeros(x_ref.shape, F32)
    for e in range(N_EXPERTS):
        q = (kt[:, e:e + 1] == lane).astype(BF16)
        acc = acc + at[:, e:e + 1] * jnp.dot(q, y_ref[e], preferred_element_type=F32)
    x = x_ref[...] + g2_ref[...] * acc
    if final:
        x = x * lax.rsqrt(jnp.mean(x * x, axis=-1, keepdims=True) + EPS) * fg_ref[...]
    o_ref[...] = x


def _combine_call(x, keyt, afft, yexp, first_row, cap, g2, final_g):
    b, n, d = x.shape
    tc = min(512, n)
    final = final_g is not None
    blk0 = first_row // cap
    row = pl.BlockSpec((None, tc, d), lambda i, j: (i, j, 0))
    tok = pl.BlockSpec((None, tc, LANES), lambda i, j: (i, j, 0))
    in_specs = [row, tok, tok,
                pl.BlockSpec((N_EXPERTS, cap, d), lambda i, j: (0, blk0 + i, 0)),
                pl.BlockSpec((None, 1, d), lambda i, j: (i, 0, 0))]
    args = [x, keyt, afft, yexp, g2]
    if final:
        in_specs.append(pl.BlockSpec((1, d), lambda i, j: (0, 0)))
        args.append(final_g.reshape(1, d))
    return pl.pallas_call(
        functools.partial(_combine_kernel, final),
        grid=(b, n // tc),
        in_specs=in_specs,
        out_specs=row,
        out_shape=jax.ShapeDtypeStruct((b, n, d), F32),
        compiler_params=_cparams(("arbitrary", "arbitrary")),
        name="combine",
    )(*args)


def _grid_pos_embed(rows, d):
    rr, cc = np.meshgrid(np.arange(rows, dtype=np.float64), np.arange(GRID_W, dtype=np.float64), indexing="ij")
    quarter = d // 4
    omega = 1.0 / (POS_BASE ** (np.arange(quarter, dtype=np.float64) / quarter))

    def enc(p):
        ang = p.reshape(-1)[:, None] * omega
        return np.concatenate([np.sin(ang), np.cos(ang)], axis=-1)

    return np.concatenate([enc(rr), enc(cc)], axis=-1).astype(np.float32)


def _moe(layer, streams, w_gate, w_up, w_down):
    d = streams[0]["x"].shape[-1]
    caps = [EC_CAPACITY * s["x"].shape[1] // N_EXPERTS for s in streams]
    row_counts = [s["x"].shape[0] * cap for s, cap in zip(streams, caps)]
    layout, _, _ = _ffn_layout(row_counts)
    idxs, keyts, affts = [], [], []
    for s, cap, (m, padded, off) in zip(streams, caps, layout):
        csum, keyt, afft = _route_call(s["logits_t"], cap)
        idx = _compact_call(keyt, csum, cap)
        idx = idx.transpose(1, 0, 2).reshape(N_EXPERTS, m)
        idxs.append(jnp.pad(idx, ((0, 0), (0, padded - m))).reshape(N_EXPERTS * padded))
        keyts.append(keyt)
        affts.append(afft)
    tables = [s["h2"] for s in streams]
    yexp = _ffn_call(layer, row_counts, idxs, tables, w_gate, w_up, w_down)
    return [_combine_call(s["x"], keyt, afft, yexp, off, cap, s["g2"], s["final_g"])
            for s, keyt, afft, cap, (_, _, off) in zip(streams, keyts, affts, caps, layout)]


def kernel(x, c, ctx, c_ctx, w_mod, b_mod, norm1_g, norm2_g, w_in, conv_w, conv_b, lru_wa, lru_ba,
           lru_wi, lru_bi, lru_lambda, w_out, w_router, w_gate, w_up, w_down, final_g):
    bsz, n, d = x.shape
    depth = w_mod.shape[0]
    pos = jnp.asarray(_grid_pos_embed(n // GRID_W, d))

    crows = jnp.zeros((SUBLANES, d), F32).at[:bsz].set(c).at[bsz].set(c_ctx)
    mod = _mod_call(crows, w_mod, b_mod)

    for l in range(depth):
        last = l == depth - 1
        mx = mod[l, :bsz].reshape(bsz, 1, N_MOD, d)
        sh1, sc1, g1, sh2, sc2, g2 = [mx[:, :, i, :] for i in range(N_MOD)]
        mc = jnp.broadcast_to(mod[l, bsz].reshape(1, 1, N_MOD, d), (bsz, 1, N_MOD, d))
        csh1, csc1, cg1, csh2, csc2, cg2 = [mc[:, :, i, :] for i in range(N_MOD)]
        w_in_bf = w_in[l]
        w_out_bf = w_out[l]
        w_router_t = w_router[l].T
        lp = _lru_params(conv_w[l], conv_b[l], lru_wa[l], lru_ba[l], lru_wi[l], lru_bi[l], lru_lambda[l])

        cuf, cux, cug = _proj_call(ctx, None, norm1_g[l], csh1, csc1, w_in_bf)
        cyr, ctx_states = _lru_call(cux, cug, lp, None)
        streams = []
        if not last:
            cyf = _fourier_call(cuf)
            ctx, ch2, clg = _mixout_call(ctx, cyf, cyr, w_out_bf, cg1, norm2_g[l], csh2, csc2, w_router_t)
            streams.append(dict(x=ctx, h2=ch2, logits_t=clg, g2=cg2, final_g=None))

        if l == 0:
            uf, ux, ug, x = _proj_call(x, pos, norm1_g[l], sh1, sc1, w_in_bf)
        else:
            uf, ux, ug = _proj_call(x, None, norm1_g[l], sh1, sc1, w_in_bf)
        yf = _fourier_call(uf)
        yr, _ = _lru_call(ux, ug, lp, ctx_states)
        x, h2, lg = _mixout_call(x, yf, yr, w_out_bf, g1, norm2_g[l], sh2, sc2, w_router_t)
        streams.insert(0, dict(x=x, h2=h2, logits_t=lg, g2=g2, final_g=final_g if last else None))
        outs = _moe(l, streams, w_gate, w_up, w_down)
        x = outs[0]
        if not last:
            ctx = outs[1]
    return x
```

```python
import functools
import math

import numpy as np
import jax
import jax.numpy as jnp
from jax import lax
from jax.experimental import pallas as pl
from jax.experimental.pallas import tpu as pltpu

D_MODEL = 1024
GRID_W = 64
POS_BASE = 10000.0
D_FOURIER = 512
FOURIER_GROUPS = 4
D_GROUP = D_FOURIER // FOURIER_GROUPS
D_LRU = 512
LRU_HEADS = 8
LRU_HEAD_DIM = D_LRU // LRU_HEADS
CONV_W = 4
LRU_C = 8.0
D_IN = D_FOURIER + 2 * D_LRU
N_EXPERTS = 16
EC_CAPACITY = 2
D_FF = 2816
N_MOD = 6
EPS = 1e-6

LANES = 128
SUBLANES = 8
DFT_BLOCK = 256
TOKEN_ROWS = D_MODEL // LANES
VMEM_LIMIT = 56 * 1024 * 1024

F32 = jnp.float32
BF16 = jnp.bfloat16


def _cparams(sem, **kw):
    return pltpu.CompilerParams(dimension_semantics=sem, vmem_limit_bytes=VMEM_LIMIT, **kw)


def _mod_kernel(c_ref, w_ref, b_ref, o_ref):
    c = c_ref[...]
    cond = c * jax.nn.sigmoid(c)
    o_ref[...] = jnp.dot(cond, w_ref[...], preferred_element_type=F32,
                         precision=lax.Precision.HIGHEST) + b_ref[...]


def _mod_call(crows, w_mod, b_mod):
    depth, d, nm = w_mod.shape
    tn = 1536
    return pl.pallas_call(
        _mod_kernel,
        grid=(depth, nm // tn),
        in_specs=[
            pl.BlockSpec((SUBLANES, d), lambda l, j: (0, 0)),
            pl.BlockSpec((None, d, tn), lambda l, j: (l, 0, j)),
            pl.BlockSpec((None, 1, tn), lambda l, j: (l, 0, j)),
        ],
        out_specs=pl.BlockSpec((None, SUBLANES, tn), lambda l, j: (l, 0, j)),
        out_shape=jax.ShapeDtypeStruct((depth, SUBLANES, nm), F32),
        compiler_params=_cparams(("arbitrary", "arbitrary")),
        name="mod",
    )(crows, w_mod, b_mod.reshape(depth, 1, nm))


def _rms_mod(x, g, shift, scale):
    y = x * lax.rsqrt(jnp.mean(x * x, axis=-1, keepdims=True) + EPS) * g
    return y * (1.0 + scale) + shift


def _proj_kernel(add_pos, *refs):
    if add_pos:
        x_ref, pos_ref, g_ref, sh_ref, sc_ref, w_ref, uf_ref, ux_ref, ug_ref, xp_ref, wbf_ref = refs
        x = x_ref[...] + pos_ref[...]
        xp_ref[...] = x
    else:
        x_ref, g_ref, sh_ref, sc_ref, w_ref, uf_ref, ux_ref, ug_ref, wbf_ref = refs
        x = x_ref[...]

    @pl.when((pl.program_id(0) == 0) & (pl.program_id(1) == 0))
    def _cast_w():
        wbf_ref[...] = w_ref[...].astype(BF16)

    h = _rms_mod(x, g_ref[...], sh_ref[...], sc_ref[...])
    u = jnp.dot(h.astype(BF16), wbf_ref[...], preferred_element_type=F32)
    uf_ref[...] = u[:, :D_FOURIER]
    ux_ref[...] = u[:, D_FOURIER:D_FOURIER + D_LRU]
    ug_ref[...] = u[:, D_FOURIER + D_LRU:]


def _proj_call(x, pos, g, shift, scale, w_in_bf):
    b, n, d = x.shape
    tm = min(512, n)
    add_pos = pos is not None
    row = pl.BlockSpec((None, tm, d), lambda i, j: (i, j, 0))
    vec_b = pl.BlockSpec((None, 1, d), lambda i, j: (i, 0, 0))
    in_specs = [row]
    args = [x]
    if add_pos:
        in_specs.append(pl.BlockSpec((tm, d), lambda i, j: (j, 0)))
        args.append(pos)
    in_specs += [pl.BlockSpec((1, d), lambda i, j: (0, 0)), vec_b, vec_b,
                 pl.BlockSpec((d, D_IN), lambda i, j: (0, 0))]
    args += [g.reshape(1, d), shift, scale, w_in_bf]
    part = pl.BlockSpec((None, tm, D_FOURIER), lambda i, j: (i, j, 0))
    out_specs = [part, part, part]
    out_shape = [jax.ShapeDtypeStruct((b, n, D_FOURIER), F32)] * 3
    if add_pos:
        out_specs.append(row)
        out_shape.append(jax.ShapeDtypeStruct((b, n, d), F32))
    return pl.pallas_call(
        functools.partial(_proj_kernel, add_pos),
        grid=(b, n // tm),
        in_specs=in_specs,
        out_specs=out_specs,
        out_shape=out_shape,
        scratch_shapes=[pltpu.VMEM((d, D_IN), BF16)],
        compiler_params=_cparams(("arbitrary", "arbitrary")),
        name="proj",
    )(*args)


def _cmul_const(re, im, c, s):
    tol = 1e-12
    if abs(s) < tol:
        if abs(c - 1.0) < tol:
            return re, im
        if abs(c + 1.0) < tol:
            return -re, -im
        return re * c, im * c
    if abs(c) < tol:
        if abs(s - 1.0) < tol:
            return -im, re
        if abs(s + 1.0) < tol:
            return im, -re
        return -im * s, re * s
    return re * c - im * s, re * s + im * c


def _fft_list(xs):
    n = len(xs)
    if n == 1:
        return xs
    ev = _fft_list(xs[0::2])
    od = _fft_list(xs[1::2])
    out = [None] * n
    for k in range(n // 2):
        ang = -2.0 * math.pi * k / n
        tr, ti = _cmul_const(od[k][0], od[k][1], math.cos(ang), math.sin(ang))
        out[k] = (ev[k][0] + tr, ev[k][1] + ti)
        out[k + n // 2] = (ev[k][0] - tr, ev[k][1] - ti)
    return out


def _fourier_kernel(n1_count, u_ref, cs_ref, twc_ref, tws_ref, cc_ref, o_ref, tr_ref, ti_ref, pq_ref):
    n = u_ref.shape[0]
    cs = cs_ref[...].astype(BF16)
    for n1 in range(n1_count):
        if n1_count == 1:
            z = u_ref[...]
        else:
            z = u_ref[pl.ds(n1, DFT_BLOCK, stride=n1_count), :]
        g = jnp.dot(cs, z.astype(BF16), preferred_element_type=F32)
        gr = g[:DFT_BLOCK]
        gs = g[DFT_BLOCK:]
        if n1 == 0:
            tr_ref[n1] = gr
            ti_ref[n1] = -gs
        else:
            c = twc_ref[n1]
            s = tws_ref[n1]
            tr_ref[n1] = gr * c - gs * s
            ti_ref[n1] = -(gr * s + gs * c)

    if n1_count == 1:
        pq_ref[:, :D_GROUP] = tr_ref[0]
        pq_ref[:, D_GROUP:] = -ti_ref[0]
    else:
        def chunk(j, carry):
            r0 = pl.multiple_of(j * SUBLANES, SUBLANES)
            xs = [(tr_ref[i, pl.ds(r0, SUBLANES), :], ti_ref[i, pl.ds(r0, SUBLANES), :])
                  for i in range(n1_count)]
            ys = _fft_list(xs)
            for k1 in range(n1_count):
                rows = pl.ds(pl.multiple_of(k1 * DFT_BLOCK + r0, SUBLANES), SUBLANES)
                pq_ref[rows, :D_GROUP] = ys[k1][0]
                pq_ref[rows, D_GROUP:] = -ys[k1][1]
            return carry
        lax.fori_loop(0, DFT_BLOCK // SUBLANES, chunk, 0)

    cc = cc_ref[...].astype(BF16)
    tile = min(512, n)
    for i in range(n // tile):
        rows = pl.ds(i * tile, tile)
        o_ref[rows, :] = jnp.dot(pq_ref[rows, :].astype(BF16), cc, preferred_element_type=F32)


@functools.lru_cache(maxsize=None)
def _fourier_consts(n):
    n1c = n // DFT_BLOCK
    k = np.arange(DFT_BLOCK, dtype=np.float64)
    ang = 2.0 * np.pi * np.outer(k, k) / DFT_BLOCK
    cs = np.concatenate([np.cos(ang), np.sin(ang)], axis=0)
    tw = 2.0 * np.pi * np.outer(np.arange(n1c, dtype=np.float64), k) / n
    twc = np.repeat(np.cos(tw)[:, :, None], LANES, axis=2)
    tws = np.repeat(np.sin(tw)[:, :, None], LANES, axis=2)
    c = np.arange(D_GROUP, dtype=np.float64)
    angc = 2.0 * np.pi * np.outer(c, c) / D_GROUP
    scale = 1.0 / math.sqrt(n * D_GROUP)
    cc = np.concatenate([np.cos(angc), -np.sin(angc)], axis=0) * scale
    return (cs.astype(np.float32), twc.astype(np.float32), tws.astype(np.float32), cc.astype(np.float32))


def _fourier_call(u):
    b, n, _ = u.shape
    n1c = n // DFT_BLOCK
    cs, twc, tws, cc = _fourier_consts(n)
    blk = pl.BlockSpec((None, n, D_GROUP), lambda i, j: (i, 0, j))
    return pl.pallas_call(
        functools.partial(_fourier_kernel, n1c),
        grid=(b, FOURIER_GROUPS),
        in_specs=[
            blk,
            pl.BlockSpec((2 * DFT_BLOCK, DFT_BLOCK), lambda i, j: (0, 0)),
            pl.BlockSpec((n1c, DFT_BLOCK, LANES), lambda i, j: (0, 0, 0)),
            pl.BlockSpec((n1c, DFT_BLOCK, LANES), lambda i, j: (0, 0, 0)),
            pl.BlockSpec((2 * D_GROUP, D_GROUP), lambda i, j: (0, 0)),
        ],
        out_specs=blk,
        out_shape=jax.ShapeDtypeStruct(u.shape, F32),
        scratch_shapes=[
            pltpu.VMEM((n1c, DFT_BLOCK, D_GROUP), F32),
            pltpu.VMEM((n1c, DFT_BLOCK, D_GROUP), F32),
            pltpu.VMEM((n, 2 * D_GROUP), F32),
        ],
        compiler_params=_cparams(("arbitrary", "arbitrary")),
        name="fourier",
    )(u, jnp.asarray(cs), jnp.asarray(twc), jnp.asarray(tws), jnp.asarray(cc))


SEGMENTS = SUBLANES
PAD_ROWS = 8


def _expm1(x):
    u = jnp.exp(x)
    near = (u - 1.0) * x / jnp.log(jnp.where(u == 1.0, 2.0, u))
    return jnp.where(x < -0.5, u - 1.0, jnp.where(u == 1.0, x, near))


def _lru_kernel(has_h0, *refs):
    if has_h0:
        (ux_ref, ug_ref, cw_ref, cb_ref, wg_ref, bg_ref, lam_ref, h0_ref,
         y_ref, fin_ref, upad_ref, a_ref, b_ref, h_ref) = refs
    else:
        (ux_ref, ug_ref, cw_ref, cb_ref, wg_ref, bg_ref, lam_ref,
         y_ref, fin_ref, upad_ref, a_ref, b_ref, h_ref) = refs
    n = ux_ref.shape[0]
    seg_len = n // SEGMENTS
    tile = min(256, seg_len)

    zeros_pad = jnp.zeros((PAD_ROWS, LANES), F32)
    upad_ref[pl.ds(0, PAD_ROWS), :] = zeros_pad
    upad_ref[pl.ds(PAD_ROWS + n, PAD_ROWS), :] = zeros_pad
    upad_ref[pl.ds(PAD_ROWS, n), :] = ux_ref[...]

    lam = lam_ref[...]
    sp = jnp.maximum(-lam, 0.0) + jnp.log1p(jnp.exp(-jnp.abs(lam)))
    cw = cw_ref[...]
    cb = cb_ref[...]
    wg = wg_ref[...].astype(BF16)
    bg = bg_ref[...]
    left = (CONV_W - 1) // 2

    for t in range(n // tile):
        r0 = t * tile
        xc = cb
        for k in range(CONV_W):
            xc = xc + upad_ref[pl.ds(PAD_ROWS + r0 + k - left, tile), :] * cw[k:k + 1, :]
        gates = jnp.dot(xc.astype(BF16), wg, preferred_element_type=F32) + bg
        seg = r0 // seg_len
        s0 = r0 - seg * seg_len
        for d in range(2):
            r = jax.nn.sigmoid(gates[:, (2 * d) * LANES:(2 * d + 1) * LANES])
            gi = jax.nn.sigmoid(gates[:, (2 * d + 1) * LANES:(2 * d + 2) * LANES])
            log_a = (-LRU_C) * r * sp[d:d + 1, :]
            a = jnp.exp(log_a)
            drive = jnp.sqrt(-_expm1(2.0 * log_a)) * (gi * xc)
            dst = pl.ds(s0 * SEGMENTS + seg, tile, stride=SEGMENTS)
            a_ref[d, dst, :] = a
            b_ref[d, dst, :] = drive

    def rows(s):
        return pl.ds(pl.multiple_of(s * SEGMENTS, SEGMENTS), SEGMENTS)

    def pass1(t, carry):
        hf, af, hb, ab = carry
        sb = seg_len - 1 - t
        a0 = a_ref[0, rows(t), :]
        a1 = a_ref[1, rows(sb), :]
        hf = a0 * hf + b_ref[0, rows(t), :]
        hb = a1 * hb + b_ref[1, rows(sb), :]
        return hf, af * a0, hb, ab * a1

    z = jnp.zeros((SEGMENTS, LANES), F32)
    o = jnp.ones((SEGMENTS, LANES), F32)
    hf, af, hb, ab = lax.fori_loop(0, seg_len, pass1, (z, o, z, o), unroll=8)

    if has_h0:
        cf = h0_ref[0:1, :]
        cbk = h0_ref[1:2, :]
    else:
        cf = jnp.zeros((1, LANES), F32)
        cbk = jnp.zeros((1, LANES), F32)
    sub = lax.broadcasted_iota(jnp.int32, (SEGMENTS, LANES), 0)
    carry_f = z
    for j in range(SEGMENTS):
        carry_f = jnp.where(sub == j, cf, carry_f)
        cf = hf[j:j + 1, :] + af[j:j + 1, :] * cf
    carry_b = z
    for j in reversed(range(SEGMENTS)):
        carry_b = jnp.where(sub == j, cbk, carry_b)
        cbk = hb[j:j + 1, :] + ab[j:j + 1, :] * cbk
    fin_ref[0:1, :] = cf
    fin_ref[1:2, :] = cbk

    def pass2(t, carry):
        hf, hb = carry
        sb = seg_len - 1 - t
        hf = a_ref[0, rows(t), :] * hf + b_ref[0, rows(t), :]
        h_ref[0, rows(t), :] = hf
        hb = a_ref[1, rows(sb), :] * hb + b_ref[1, rows(sb), :]
        h_ref[1, rows(sb), :] = hb
        return hf, hb

    lax.fori_loop(0, seg_len, pass2, (carry_f, carry_b), unroll=8)

    for t in range(n // tile):
        r0 = t * tile
        seg = r0 // seg_len
        s0 = r0 - seg * seg_len
        src = pl.ds(s0 * SEGMENTS + seg, tile, stride=SEGMENTS)
        hsum = h_ref[0, src, :] + h_ref[1, src, :]
        y_ref[pl.ds(r0, tile), :] = hsum * jax.nn.gelu(ug_ref[pl.ds(r0, tile), :])


def _lru_call(ux, ug, lp, h0):
    b, n, _ = ux.shape
    nblk = D_LRU // LANES
    has_h0 = h0 is not None
    blk = pl.BlockSpec((None, n, LANES), lambda i, j: (i, 0, j))
    in_specs = [
        blk, blk,
        pl.BlockSpec((CONV_W, LANES), lambda i, j: (0, j)),
        pl.BlockSpec((1, LANES), lambda i, j: (0, j)),
        pl.BlockSpec((None, LANES, 4 * LANES), lambda i, j: (j, 0, 0)),
        pl.BlockSpec((None, 1, 4 * LANES), lambda i, j: (j, 0, 0)),
        pl.BlockSpec((2, LANES), lambda i, j: (0, j)),
    ]
    args = [ux, ug, lp["conv_w"], lp["conv_b"], lp["w_gates"], lp["b_gates"], lp["lam"]]
    if has_h0:
        in_specs.append(pl.BlockSpec((None, 2, LANES), lambda i, j: (i, 0, j)))
        args.append(h0)
    return pl.pallas_call(
        functools.partial(_lru_kernel, has_h0),
        grid=(b, nblk),
        in_specs=in_specs,
        out_specs=[blk, pl.BlockSpec((None, 2, LANES), lambda i, j: (i, 0, j))],
        out_shape=[jax.ShapeDtypeStruct((b, n, D_LRU), F32),
                   jax.ShapeDtypeStruct((b, 2, D_LRU), F32)],
        scratch_shapes=[
            pltpu.VMEM((n + 2 * PAD_ROWS, LANES), F32),
            pltpu.VMEM((2, n, LANES), F32),
            pltpu.VMEM((2, n, LANES), F32),
            pltpu.VMEM((2, n, LANES), F32),
        ],
        compiler_params=_cparams(("arbitrary", "arbitrary")),
        name="lru",
    )(*args)


def _lru_params(conv_w, conv_b, wa, ba, wi, bi, lam):
    nblk = D_LRU // LANES
    hpb = LANES // LRU_HEAD_DIM

    def blockdiag(w):
        w = w.reshape(nblk, hpb, LRU_HEAD_DIM, LRU_HEAD_DIM)
        eye = jnp.eye(hpb, dtype=w.dtype)
        return jnp.einsum("bhij,hg->bhigj", w, eye).reshape(nblk, LANES, LANES)

    w_gates = jnp.concatenate([blockdiag(wa[0]), blockdiag(wi[0]), blockdiag(wa[1]), blockdiag(wi[1])], axis=-1)
    b_gates = jnp.stack([ba[0], bi[0], ba[1], bi[1]], axis=0)
    b_gates = b_gates.reshape(4, nblk, LANES).transpose(1, 0, 2).reshape(nblk, 1, 4 * LANES)
    return {"conv_w": conv_w, "conv_b": conv_b.reshape(1, D_LRU), "w_gates": w_gates,
            "b_gates": b_gates, "lam": lam}


def _mixout_kernel(x_ref, yf_ref, yr_ref, w_ref, g1_ref, n2_ref, sh_ref, sc_ref, wr_ref,
                   xo_ref, h2_ref, lg_ref, wbf_ref):
    @pl.when((pl.program_id(0) == 0) & (pl.program_id(1) == 0))
    def _cast_w():
        wbf_ref[...] = w_ref[...].astype(BF16)

    mix = jnp.dot(yf_ref[...].astype(BF16), wbf_ref[:D_FOURIER, :], preferred_element_type=F32)
    mix = mix + jnp.dot(yr_ref[...].astype(BF16), wbf_ref[D_FOURIER:, :], preferred_element_type=F32)
    x = x_ref[...] + g1_ref[...] * mix
    xo_ref[...] = x
    h2 = _rms_mod(x, n2_ref[...], sh_ref[...], sc_ref[...])
    tm = h2.shape[0]
    for s in range(TOKEN_ROWS):
        h2_ref[pl.ds(s, tm, stride=TOKEN_ROWS), :] = h2[:, s * LANES:(s + 1) * LANES]
    lg_ref[...] = lax.dot_general(wr_ref[...], h2, (((1,), (1,)), ((), ())),
                                  preferred_element_type=F32, precision=lax.Precision.HIGHEST)


def _mixout_call(x, yf, yr, w_out_bf, g1, n2g, sh2, sc2, w_router_t):
    b, n, d = x.shape
    tm = min(512, n)
    row = pl.BlockSpec((None, tm, d), lambda i, j: (i, j, 0))
    half = pl.BlockSpec((None, tm, D_FOURIER), lambda i, j: (i, j, 0))
    vec_b = pl.BlockSpec((None, 1, d), lambda i, j: (i, 0, 0))
    return pl.pallas_call(
        _mixout_kernel,
        grid=(b, n // tm),
        in_specs=[row, half, half,
                  pl.BlockSpec((d, d), lambda i, j: (0, 0)),
                  vec_b,
                  pl.BlockSpec((1, d), lambda i, j: (0, 0)),
                  vec_b, vec_b,
                  pl.BlockSpec((N_EXPERTS, d), lambda i, j: (0, 0))],
        out_specs=[row,
                   pl.BlockSpec((tm * TOKEN_ROWS, LANES), lambda i, j: (i * (n // tm) + j, 0)),
                   pl.BlockSpec((None, N_EXPERTS, tm), lambda i, j: (i, 0, j))],
        out_shape=[jax.ShapeDtypeStruct((b, n, d), F32),
                   jax.ShapeDtypeStruct((b * n * TOKEN_ROWS, LANES), F32),
                   jax.ShapeDtypeStruct((b, N_EXPERTS, n), F32)],
        scratch_shapes=[pltpu.VMEM((d, d), BF16)],
        compiler_params=_cparams(("arbitrary", "arbitrary")),
        name="mixout",
    )(x, yf, yr, w_out_bf, g1, n2g.reshape(1, d), sh2, sc2, w_router_t)


def _lane_prefix_sum(x):
    n = x.shape[-1]
    lane = lax.broadcasted_iota(jnp.int32, x.shape, x.ndim - 1)
    s = 1
    while s < n:
        x = x + jnp.where(lane >= s, pltpu.roll(x, s, x.ndim - 1), 0)
        s *= 2
    return x


def _route_kernel(cap, lg_ref, key_ref, csum_ref, keyt_ref, afft_ref):
    n = lg_ref.shape[1]
    lg = lg_ref[...]
    m = jnp.max(lg, axis=0, keepdims=True)
    ex = jnp.exp(lg - m)
    aff = ex / jnp.sum(ex, axis=0, keepdims=True)
    bits = pltpu.bitcast(aff, jnp.int32)

    def bis(_, carry):
        lo, hi = carry
        mid = lo + ((hi - lo + 1) >> 1)
        cnt = jnp.sum((bits >= mid).astype(jnp.int32), axis=1, keepdims=True)
        ok = cnt >= cap
        return jnp.where(ok, mid, lo), jnp.where(ok, hi, mid - 1)

    lo0 = jnp.zeros((N_EXPERTS, 1), jnp.int32)
    hi0 = jnp.full((N_EXPERTS, 1), 0x7F800000, jnp.int32)
    thr, _ = lax.fori_loop(0, 32, bis, (lo0, hi0))

    thr_f = pltpu.bitcast(thr, F32)
    window = 2.0 ** -6
    flo0 = thr_f * (1.0 - window)
    fhi0 = jnp.maximum(thr_f * (1.0 + window), jnp.float32(1e-37))

    def fbis(_, carry):
        lo, hi = carry
        mid = lo + 0.5 * (hi - lo)
        cnt = jnp.sum((aff >= mid).astype(jnp.int32), axis=1, keepdims=True)
        ok = cnt >= cap
        return jnp.where(ok, mid, lo), jnp.where(ok, hi, mid)

    flo, fhi = lax.fori_loop(0, 32, fbis, (flo0, fhi0))
    gt = aff >= fhi
    eq = (aff >= flo) & (aff < fhi)
    need = cap - jnp.sum(gt.astype(jnp.int32), axis=1, keepdims=True)
    eq_rank = _lane_prefix_sum(eq.astype(jnp.int32)) - eq.astype(jnp.int32)
    sel = gt | (eq & (eq_rank < need))
    csum = _lane_prefix_sum(sel.astype(jnp.int32))
    key = jnp.where(sel, csum - 1, -1)
    key_ref[...] = key
    csum_ref[...] = csum

    pad = jnp.zeros((LANES - N_EXPERTS, n), F32)
    keyt_ref[...] = jnp.concatenate([key.astype(F32), pad - 1.0], axis=0).T
    afft_ref[...] = jnp.concatenate([aff, pad], axis=0).T


def _route_call(logits_t, cap):
    b, _, n = logits_t.shape
    en = pl.BlockSpec((None, N_EXPERTS, n), lambda i: (i, 0, 0))
    tm = pl.BlockSpec((None, n, LANES), lambda i: (i, 0, 0))
    return pl.pallas_call(
        functools.partial(_route_kernel, cap),
        grid=(b,),
        in_specs=[en],
        out_specs=[en, en, tm, tm],
        out_shape=[jax.ShapeDtypeStruct((b, N_EXPERTS, n), jnp.int32),
                   jax.ShapeDtypeStruct((b, N_EXPERTS, n), jnp.int32),
                   jax.ShapeDtypeStruct((b, n, LANES), F32),
                   jax.ShapeDtypeStruct((b, n, LANES), F32)],
        compiler_params=_cparams(("arbitrary",)),
        name="route",
    )(logits_t)


COMPACT_TILE = 64


def _compact_kernel(cap, klo_ref, khi_ref, key_ref, idx_ref):
    b = pl.program_id(0)
    nblk = key_ref.shape[1]
    rt = min(COMPACT_TILE, cap)
    ntile = cap // rt
    width = min(LANES, cap)
    per_store = width // rt
    lane = lax.broadcasted_iota(jnp.int32, (1, LANES), 1)
    sub = lax.broadcasted_iota(jnp.int32, (rt, 1), 0)
    base = b * (nblk * LANES)
    zero = jnp.zeros((rt, LANES), F32)
    for e in range(N_EXPERTS):
        row = jnp.zeros((1, LANES), F32)
        for t in range(ntile):
            s = (b * N_EXPERTS + e) * ntile + t
            rr = t * rt + sub

            def block(k, ids, e=e, rr=rr):
                p = key_ref[e, pl.ds(k, 1), :] == rr
                return ids + jnp.where(p, (base + k * LANES + lane).astype(F32), 0.0)

            ids = lax.fori_loop(klo_ref[s], khi_ref[s], block, zero)
            col = jnp.sum(ids, axis=1, keepdims=True)
            diag = (sub + (t % per_store) * rt) == lane
            row = row + jnp.sum(jnp.where(diag, col, 0.0), axis=0, keepdims=True)
            if (t + 1) % per_store == 0:
                g = t // per_store
                idx_ref[e:e + 1, g * width:(g + 1) * width] = row[:, :width].astype(jnp.int32)
                row = jnp.zeros((1, LANES), F32)


def _compact_call(key, csum, cap):
    b, _, n = key.shape
    nblk = n // LANES
    rt = min(COMPACT_TILE, cap)
    ntile = cap // rt
    cb_incl = csum[:, :, LANES - 1::LANES]
    cb_excl = jnp.concatenate([jnp.zeros_like(cb_incl[..., :1]), cb_incl[..., :-1]], axis=-1)
    t0 = (jnp.arange(ntile, dtype=jnp.int32) * rt)[:, None]
    klo = jnp.sum((cb_incl[:, :, None, :] <= t0).astype(jnp.int32), axis=-1)
    khi = jnp.sum((cb_excl[:, :, None, :] < t0 + rt).astype(jnp.int32), axis=-1)
    return pl.pallas_call(
        functools.partial(_compact_kernel, cap),
        grid_spec=pltpu.PrefetchScalarGridSpec(
            num_scalar_prefetch=2, grid=(b,),
            in_specs=[pl.BlockSpec((None, N_EXPERTS, nblk, LANES), lambda i, *_: (i, 0, 0, 0))],
            out_specs=pl.BlockSpec((None, N_EXPERTS, cap), lambda i, *_: (i, 0, 0))),
        out_shape=jax.ShapeDtypeStruct((b, N_EXPERTS, cap), jnp.int32),
        compiler_params=_cparams(("arbitrary",)),
        name="compact",
    )(klo.reshape(-1), khi.reshape(-1), key.reshape(b, N_EXPERTS, nblk, LANES))


FF_TILE = 256


FF_STEPS = D_FF // FF_TILE
ROW_ALIGN = 32


def _round_up(v, m):
    return -(-v // m) * m


def _ffn_layout(row_counts):
    layout, off = [], 0
    for m in row_counts:
        padded = _round_up(m, FF_STEPS * SUBLANES)
        layout.append((m, padded, off))
        off = _round_up(off + padded, ROW_ALIGN)
    out_rows = _round_up(layout[-1][2] + layout[-1][0], ROW_ALIGN)
    scratch_rows = _round_up(layout[-1][2] + layout[-1][1], SUBLANES)
    return tuple(layout), out_rows, max(scratch_rows, out_rows)


def _ffn_kernel(layout, *refs):
    ns = len(layout)
    idx_refs = refs[:ns]
    tabs = refs[ns:2 * ns]
    wg_ref, wu_ref, wd_ref, o_ref, xg32_ref, xg_ref, acc_ref, sem = refs[2 * ns:]
    e = pl.program_id(0)
    f = pl.program_id(1)
    n_e = pl.num_programs(0)
    slot = e % 2

    def issue_chunk(expert, dst_slot, step):
        for s, (_, padded, off) in enumerate(layout):
            chunk = padded // FF_STEPS
            src0 = expert * padded + step * chunk
            dst0 = off + step * chunk

            def body(g, carry, s=s, src0=src0, dst0=dst0):
                src = src0 + g * SUBLANES
                dst = pl.multiple_of(dst0 + g * SUBLANES, SUBLANES)
                for j in range(SUBLANES):
                    tok = idx_refs[s][src + j]
                    pltpu.make_async_copy(
                        tabs[s].at[pl.ds(pl.multiple_of(tok * TOKEN_ROWS, TOKEN_ROWS), TOKEN_ROWS), :],
                        xg32_ref.at[dst_slot, pl.ds(pl.multiple_of((dst + j) * TOKEN_ROWS, TOKEN_ROWS), TOKEN_ROWS), :],
                        sem.at[dst_slot]).start()
                return carry
            lax.fori_loop(0, chunk // SUBLANES, body, 0)

    @pl.when((e == 0) & (f == 0))
    def _first_expert():
        for step in range(FF_STEPS):
            issue_chunk(0, 0, step)

    @pl.when(f == 0)
    def _start_expert():
        for s, (m, padded, off) in enumerate(layout):
            pltpu.make_async_copy(tabs[s].at[pl.ds(0, padded * TOKEN_ROWS), :],
                                  xg32_ref.at[slot, pl.ds(off * TOKEN_ROWS, padded * TOKEN_ROWS), :],
                                  sem.at[slot]).wait()
        pack = 2 * SUBLANES
        for m, _, off in layout:
            def unpack(g, carry, off=off):
                t0 = pl.multiple_of(off + g * pack, pack)
                for c in range(TOKEN_ROWS):
                    v = xg32_ref[slot, pl.ds(t0 * TOKEN_ROWS + c, pack, stride=TOKEN_ROWS), :]
                    xg_ref[pl.ds(t0, pack), c * LANES:(c + 1) * LANES] = v.astype(BF16)
                return carry
            lax.fori_loop(0, m // pack, unpack, 0)
            acc_ref[pl.ds(off, m), :] = jnp.zeros((m, acc_ref.shape[1]), F32)

    nxt = jnp.minimum(e + 1, n_e - 1)
    other = 1 - slot
    copies = []
    for s, (_, padded, off) in enumerate(layout):
        chunk = padded // FF_STEPS
        for r in range(chunk):
            copies.append((s, nxt * padded + f * chunk + r, off + f * chunk, r))
    tiles = [(off + i * min(512, m), min(512, m)) for m, _, off in layout for i in range(m // min(512, m))]
    share = -(-len(copies) // len(tiles))

    wg = wg_ref[...].astype(BF16)
    wu = wu_ref[...].astype(BF16)
    wd = wd_ref[...].astype(BF16)
    for ti, (r0, mt) in enumerate(tiles):
        for s, src, dst0, r in copies[ti * share:(ti + 1) * share]:
            tok = idx_refs[s][src]
            dst = pl.multiple_of(dst0 * TOKEN_ROWS, SUBLANES * TOKEN_ROWS) + r * TOKEN_ROWS
            pltpu.make_async_copy(
                tabs[s].at[pl.ds(pl.multiple_of(tok * TOKEN_ROWS, TOKEN_ROWS), TOKEN_ROWS), :],
                xg32_ref.at[other, pl.ds(pl.multiple_of(dst, TOKEN_ROWS), TOKEN_ROWS), :],
                sem.at[other]).start()
        rows = pl.ds(r0, mt)
        xg = xg_ref[rows, :]
        g = jnp.dot(xg, wg, preferred_element_type=F32)
        u = jnp.dot(xg, wu, preferred_element_type=F32)
        hid = (g * jax.nn.sigmoid(g) * u).astype(BF16)
        acc_ref[rows, :] += jnp.dot(hid, wd, preferred_element_type=F32)

    @pl.when((e == n_e - 1) & (f == FF_STEPS - 1))
    def _drain():
        for s, (_, padded, off) in enumerate(layout):
            pltpu.make_async_copy(tabs[s].at[pl.ds(0, padded * TOKEN_ROWS), :],
                                  xg32_ref.at[other, pl.ds(off * TOKEN_ROWS, padded * TOKEN_ROWS), :],
                                  sem.at[other]).wait()

    @pl.when(f == FF_STEPS - 1)
    def _finish():
        end = 0
        for m, _, off in layout:
            if off > end:
                o_ref[pl.ds(end, off - end), :] = jnp.zeros((off - end, o_ref.shape[1]), o_ref.dtype)
            o_ref[pl.ds(off, m), :] = acc_ref[pl.ds(off, m), :].astype(o_ref.dtype)
            end = off + m
        if o_ref.shape[0] > end:
            o_ref[pl.ds(end, o_ref.shape[0] - end), :] = jnp.zeros((o_ref.shape[0] - end, o_ref.shape[1]), o_ref.dtype)


def _ffn_call(layer, row_counts, idxs, tables, w_gate, w_up, w_down):
    ns = len(tables)
    n_e = w_gate.shape[1]
    d = w_gate.shape[2]
    layout, out_rows, scratch_rows = _ffn_layout(row_counts)
    grid_spec = pltpu.PrefetchScalarGridSpec(
        num_scalar_prefetch=ns,
        grid=(n_e, FF_STEPS),
        in_specs=[pl.BlockSpec(memory_space=pl.ANY)] * ns
        + [pl.BlockSpec((None, None, d, FF_TILE), lambda e, f, *_: (layer, e, 0, f)),
           pl.BlockSpec((None, None, d, FF_TILE), lambda e, f, *_: (layer, e, 0, f)),
           pl.BlockSpec((None, None, FF_TILE, d), lambda e, f, *_: (layer, e, f, 0))],
        out_specs=pl.BlockSpec((None, out_rows, d), lambda e, f, *_: (e, 0, 0)),
        scratch_shapes=[pltpu.VMEM((2, scratch_rows * TOKEN_ROWS, LANES), F32), pltpu.VMEM((scratch_rows, d), BF16),
                        pltpu.VMEM((scratch_rows, d), F32), pltpu.SemaphoreType.DMA((2,))],
    )
    return pl.pallas_call(
        functools.partial(_ffn_kernel, layout),
        grid_spec=grid_spec,
        out_shape=jax.ShapeDtypeStruct((n_e, out_rows, d), BF16),
        compiler_params=_cparams(("arbitrary", "arbitrary")),
        name="ffn",
    )(*idxs, *tables, w_gate, w_up, w_down)


def _combine_kernel(final, *refs):
    if final:
        x_ref, kt_ref, at_ref, y_ref, g2_ref, fg_ref, o_ref = refs
    else:
        x_ref, kt_ref, at_ref, y_ref, g2_ref, o_ref = refs
    cap = y_ref.shape[1]
    kt = kt_ref[...]
    at = at_ref[...]
    lane = lax.broadcasted_iota(jnp.int32, (1, cap), 1).astype(F32)
    acc = jnp.zeros(x_ref.shape, F32)
    for e in range(N_EXPERTS):
        q = (kt[:, e:e + 1] == lane).astype(BF16)
        acc = acc + at[:, e:e + 1] * jnp.dot(q, y_ref[e], preferred_element_type=F32)
    x = x_ref[...] + g2_ref[...] * acc
    if final:
        x = x * lax.rsqrt(jnp.mean(x * x, axis=-1, keepdims=True) + EPS) * fg_ref[...]
    o_ref[...] = x


def _combine_call(x, keyt, afft, yexp, first_row, cap, g2, final_g):
    b, n, d = x.shape
    tc = min(512, n)
    final = final_g is not None
    blk0 = first_row // cap
    row = pl.BlockSpec((None, tc, d), lambda i, j: (i, j, 0))
    tok = pl.BlockSpec((None, tc, LANES), lambda i, j: (i, j, 0))
    in_specs = [row, tok, tok,
                pl.BlockSpec((N_EXPERTS, cap, d), lambda i, j: (0, blk0 + i, 0)),
                pl.BlockSpec((None, 1, d), lambda i, j: (i, 0, 0))]
    args = [x, keyt, afft, yexp, g2]
    if final:
        in_specs.append(pl.BlockSpec((1, d), lambda i, j: (0, 0)))
        args.append(final_g.reshape(1, d))
    return pl.pallas_call(
        functools.partial(_combine_kernel, final),
        grid=(b, n // tc),
        in_specs=in_specs,
        out_specs=row,
        out_shape=jax.ShapeDtypeStruct((b, n, d), F32),
        compiler_params=_cparams(("arbitrary", "arbitrary")),
        name="combine",
    )(*args)


def _grid_pos_embed(rows, d):
    rr, cc = np.meshgrid(np.arange(rows, dtype=np.float64), np.arange(GRID_W, dtype=np.float64), indexing="ij")
    quarter = d // 4
    omega = 1.0 / (POS_BASE ** (np.arange(quarter, dtype=np.float64) / quarter))

    def enc(p):
        ang = p.reshape(-1)[:, None] * omega
        return np.concatenate([np.sin(ang), np.cos(ang)], axis=-1)

    return np.concatenate([enc(rr), enc(cc)], axis=-1).astype(np.float32)


def _moe(layer, streams, w_gate, w_up, w_down):
    d = streams[0]["x"].shape[-1]
    caps = [EC_CAPACITY * s["x"].shape[1] // N_EXPERTS for s in streams]
    row_counts = [s["x"].shape[0] * cap for s, cap in zip(streams, caps)]
    layout, _, _ = _ffn_layout(row_counts)
    idxs, keyts, affts = [], [], []
    for s, cap, (m, padded, off) in zip(streams, caps, layout):
        key, csum, keyt, afft = _route_call(s["logits_t"], cap)
        idx = _compact_call(key, csum, cap)
        idx = idx.transpose(1, 0, 2).reshape(N_EXPERTS, m)
        idxs.append(jnp.pad(idx, ((0, 0), (0, padded - m))).reshape(N_EXPERTS * padded))
        keyts.append(keyt)
        affts.append(afft)
    tables = [s["h2"] for s in streams]
    yexp = _ffn_call(layer, row_counts, idxs, tables, w_gate, w_up, w_down)
    return [_combine_call(s["x"], keyt, afft, yexp, off, cap, s["g2"], s["final_g"])
            for s, keyt, afft, cap, (_, _, off) in zip(streams, keyts, affts, caps, layout)]


def kernel(x, c, ctx, c_ctx, w_mod, b_mod, norm1_g, norm2_g, w_in, conv_w, conv_b, lru_wa, lru_ba,
           lru_wi, lru_bi, lru_lambda, w_out, w_router, w_gate, w_up, w_down, final_g):
    bsz, n, d = x.shape
    depth = w_mod.shape[0]
    pos = jnp.asarray(_grid_pos_embed(n // GRID_W, d))

    crows = jnp.zeros((SUBLANES, d), F32).at[:bsz].set(c).at[bsz].set(c_ctx)
    mod = _mod_call(crows, w_mod, b_mod)

    for l in range(depth):
        last = l == depth - 1
        mx = mod[l, :bsz].reshape(bsz, 1, N_MOD, d)
        sh1, sc1, g1, sh2, sc2, g2 = [mx[:, :, i, :] for i in range(N_MOD)]
        mc = jnp.broadcast_to(mod[l, bsz].reshape(1, 1, N_MOD, d), (bsz, 1, N_MOD, d))
        csh1, csc1, cg1, csh2, csc2, cg2 = [mc[:, :, i, :] for i in range(N_MOD)]
        w_in_bf = w_in[l]
        w_out_bf = w_out[l]
        w_router_t = w_router[l].T
        lp = _lru_params(conv_w[l], conv_b[l], lru_wa[l], lru_ba[l], lru_wi[l], lru_bi[l], lru_lambda[l])

        cuf, cux, cug = _proj_call(ctx, None, norm1_g[l], csh1, csc1, w_in_bf)
        cyr, ctx_states = _lru_call(cux, cug, lp, None)
        streams = []
        if not last:
            cyf = _fourier_call(cuf)
            ctx, ch2, clg = _mixout_call(ctx, cyf, cyr, w_out_bf, cg1, norm2_g[l], csh2, csc2, w_router_t)
            streams.append(dict(x=ctx, h2=ch2, logits_t=clg, g2=cg2, final_g=None))

        if l == 0:
            uf, ux, ug, x = _proj_call(x, pos, norm1_g[l], sh1, sc1, w_in_bf)
        else:
            uf, ux, ug = _proj_call(x, None, norm1_g[l], sh1, sc1, w_in_bf)
        yf = _fourier_call(uf)
        yr, _ = _lru_call(ux, ug, lp, ctx_states)
        x, h2, lg = _mixout_call(x, yf, yr, w_out_bf, g1, norm2_g[l], sh2, sc2, w_router_t)
        streams.insert(0, dict(x=x, h2=h2, logits_t=lg, g2=g2, final_g=final_g if last else None))
        outs = _moe(l, streams, w_gate, w_up, w_down)
        x = outs[0]
        if not last:
            ctx = outs[1]
    return x
```

```python
import functools
import math

import numpy as np
import jax
import jax.numpy as jnp
from jax import lax
from jax.experimental import pallas as pl
from jax.experimental.pallas import tpu as pltpu

D_MODEL = 1024
GRID_W = 64
POS_BASE = 10000.0
D_FOURIER = 512
FOURIER_GROUPS = 4
D_GROUP = D_FOURIER // FOURIER_GROUPS
D_LRU = 512
LRU_HEADS = 8
LRU_HEAD_DIM = D_LRU // LRU_HEADS
CONV_W = 4
LRU_C = 8.0
D_IN = D_FOURIER + 2 * D_LRU
N_EXPERTS = 16
EC_CAPACITY = 2
D_FF = 2816
N_MOD = 6
EPS = 1e-6

LANES = 128
SUBLANES = 8
DFT_BLOCK = 256
TOKEN_ROWS = D_MODEL // LANES
VMEM_LIMIT = 56 * 1024 * 1024

F32 = jnp.float32
BF16 = jnp.bfloat16


def _cparams(sem, **kw):
    return pltpu.CompilerParams(dimension_semantics=sem, vmem_limit_bytes=VMEM_LIMIT, **kw)


def _mod_kernel(c_ref, w_ref, b_ref, o_ref):
    c = c_ref[...]
    cond = c * jax.nn.sigmoid(c)
    o_ref[...] = jnp.dot(cond, w_ref[...], preferred_element_type=F32,
                         precision=lax.Precision.HIGHEST) + b_ref[...]


def _mod_call(crows, w_mod, b_mod):
    depth, d, nm = w_mod.shape
    tn = 1536
    return pl.pallas_call(
        _mod_kernel,
        grid=(depth, nm // tn),
        in_specs=[
            pl.BlockSpec((SUBLANES, d), lambda l, j: (0, 0)),
            pl.BlockSpec((None, d, tn), lambda l, j: (l, 0, j)),
            pl.BlockSpec((None, 1, tn), lambda l, j: (l, 0, j)),
        ],
        out_specs=pl.BlockSpec((None, SUBLANES, tn), lambda l, j: (l, 0, j)),
        out_shape=jax.ShapeDtypeStruct((depth, SUBLANES, nm), F32),
        compiler_params=_cparams(("arbitrary", "arbitrary")),
        name="mod",
    )(crows, w_mod, b_mod.reshape(depth, 1, nm))


def _rms_mod(x, g, shift, scale):
    y = x * lax.rsqrt(jnp.mean(x * x, axis=-1, keepdims=True) + EPS) * g
    return y * (1.0 + scale) + shift


def _proj_kernel(add_pos, *refs):
    if add_pos:
        x_ref, pos_ref, g_ref, sh_ref, sc_ref, w_ref, uf_ref, ux_ref, ug_ref, xp_ref, wbf_ref = refs
        x = x_ref[...] + pos_ref[...]
        xp_ref[...] = x
    else:
        x_ref, g_ref, sh_ref, sc_ref, w_ref, uf_ref, ux_ref, ug_ref, wbf_ref = refs
        x = x_ref[...]

    @pl.when((pl.program_id(0) == 0) & (pl.program_id(1) == 0))
    def _cast_w():
        wbf_ref[...] = w_ref[...].astype(BF16)

    h = _rms_mod(x, g_ref[...], sh_ref[...], sc_ref[...])
    u = jnp.dot(h.astype(BF16), wbf_ref[...], preferred_element_type=F32)
    uf_ref[...] = u[:, :D_FOURIER]
    ux_ref[...] = u[:, D_FOURIER:D_FOURIER + D_LRU]
    ug_ref[...] = u[:, D_FOURIER + D_LRU:]


def _layer_weight_spec(w, layer):
    rows, cols = w.shape[-2:]
    if w.ndim == 2:
        return pl.BlockSpec((rows, cols), lambda i, j: (0, 0))
    return pl.BlockSpec((None, rows, cols), lambda i, j: (layer, 0, 0))


def _proj_call(x, pos, g, shift, scale, w_in, layer=0):
    b, n, d = x.shape
    tm = min(512, n)
    add_pos = pos is not None
    row = pl.BlockSpec((None, tm, d), lambda i, j: (i, j, 0))
    vec_b = pl.BlockSpec((None, 1, d), lambda i, j: (i, 0, 0))
    in_specs = [row]
    args = [x]
    if add_pos:
        in_specs.append(pl.BlockSpec((tm, d), lambda i, j: (j, 0)))
        args.append(pos)
    in_specs += [pl.BlockSpec((1, d), lambda i, j: (0, 0)), vec_b, vec_b,
                 _layer_weight_spec(w_in, layer)]
    args += [g.reshape(1, d), shift, scale, w_in]
    part = pl.BlockSpec((None, tm, D_FOURIER), lambda i, j: (i, j, 0))
    out_specs = [part, part, part]
    out_shape = [jax.ShapeDtypeStruct((b, n, D_FOURIER), F32)] * 3
    if add_pos:
        out_specs.append(row)
        out_shape.append(jax.ShapeDtypeStruct((b, n, d), F32))
    return pl.pallas_call(
        functools.partial(_proj_kernel, add_pos),
        grid=(b, n // tm),
        in_specs=in_specs,
        out_specs=out_specs,
        out_shape=out_shape,
        scratch_shapes=[pltpu.VMEM((d, D_IN), BF16)],
        compiler_params=_cparams(("arbitrary", "arbitrary")),
        name="proj",
    )(*args)


def _cmul_const(re, im, c, s):
    tol = 1e-12
    if abs(s) < tol:
        if abs(c - 1.0) < tol:
            return re, im
        if abs(c + 1.0) < tol:
            return -re, -im
        return re * c, im * c
    if abs(c) < tol:
        if abs(s - 1.0) < tol:
            return -im, re
        if abs(s + 1.0) < tol:
            return im, -re
        return -im * s, re * s
    return re * c - im * s, re * s + im * c


def _fft_list(xs):
    n = len(xs)
    if n == 1:
        return xs
    ev = _fft_list(xs[0::2])
    od = _fft_list(xs[1::2])
    out = [None] * n
    for k in range(n // 2):
        ang = -2.0 * math.pi * k / n
        tr, ti = _cmul_const(od[k][0], od[k][1], math.cos(ang), math.sin(ang))
        out[k] = (ev[k][0] + tr, ev[k][1] + ti)
        out[k + n // 2] = (ev[k][0] - tr, ev[k][1] - ti)
    return out


def _fourier_kernel(n1_count, u_ref, cs_ref, twc_ref, tws_ref, cc_ref, o_ref, tr_ref, ti_ref, pq_ref):
    n = u_ref.shape[0]
    cs = cs_ref[...].astype(BF16)
    for n1 in range(n1_count):
        if n1_count == 1:
            z = u_ref[...]
        else:
            z = u_ref[pl.ds(n1, DFT_BLOCK, stride=n1_count), :]
        g = jnp.dot(cs, z.astype(BF16), preferred_element_type=F32)
        gr = g[:DFT_BLOCK]
        gs = g[DFT_BLOCK:]
        if n1 == 0:
            tr_ref[n1] = gr
            ti_ref[n1] = -gs
        else:
            c = twc_ref[n1]
            s = tws_ref[n1]
            tr_ref[n1] = gr * c - gs * s
            ti_ref[n1] = -(gr * s + gs * c)

    if n1_count == 1:
        pq_ref[:, :D_GROUP] = tr_ref[0]
        pq_ref[:, D_GROUP:] = -ti_ref[0]
    else:
        def chunk(j, carry):
            r0 = pl.multiple_of(j * SUBLANES, SUBLANES)
            xs = [(tr_ref[i, pl.ds(r0, SUBLANES), :], ti_ref[i, pl.ds(r0, SUBLANES), :])
                  for i in range(n1_count)]
            ys = _fft_list(xs)
            for k1 in range(n1_count):
                rows = pl.ds(pl.multiple_of(k1 * DFT_BLOCK + r0, SUBLANES), SUBLANES)
                pq_ref[rows, :D_GROUP] = ys[k1][0]
                pq_ref[rows, D_GROUP:] = -ys[k1][1]
            return carry
        lax.fori_loop(0, DFT_BLOCK // SUBLANES, chunk, 0)

    cc = cc_ref[...].astype(BF16)
    tile = min(512, n)
    for i in range(n // tile):
        rows = pl.ds(i * tile, tile)
        o_ref[rows, :] = jnp.dot(pq_ref[rows, :].astype(BF16), cc, preferred_element_type=F32)


@functools.lru_cache(maxsize=None)
def _fourier_consts(n):
    n1c = n // DFT_BLOCK
    k = np.arange(DFT_BLOCK, dtype=np.float64)
    ang = 2.0 * np.pi * np.outer(k, k) / DFT_BLOCK
    cs = np.concatenate([np.cos(ang), np.sin(ang)], axis=0)
    tw = 2.0 * np.pi * np.outer(np.arange(n1c, dtype=np.float64), k) / n
    twc = np.repeat(np.cos(tw)[:, :, None], LANES, axis=2)
    tws = np.repeat(np.sin(tw)[:, :, None], LANES, axis=2)
    c = np.arange(D_GROUP, dtype=np.float64)
    angc = 2.0 * np.pi * np.outer(c, c) / D_GROUP
    scale = 1.0 / math.sqrt(n * D_GROUP)
    cc = np.concatenate([np.cos(angc), -np.sin(angc)], axis=0) * scale
    return (cs.astype(np.float32), twc.astype(np.float32), tws.astype(np.float32), cc.astype(np.float32))


def _fourier_call(u):
    b, n, _ = u.shape
    n1c = n // DFT_BLOCK
    cs, twc, tws, cc = _fourier_consts(n)
    blk = pl.BlockSpec((None, n, D_GROUP), lambda i, j: (i, 0, j))
    return pl.pallas_call(
        functools.partial(_fourier_kernel, n1c),
        grid=(b, FOURIER_GROUPS),
        in_specs=[
            blk,
            pl.BlockSpec((2 * DFT_BLOCK, DFT_BLOCK), lambda i, j: (0, 0)),
            pl.BlockSpec((n1c, DFT_BLOCK, LANES), lambda i, j: (0, 0, 0)),
            pl.BlockSpec((n1c, DFT_BLOCK, LANES), lambda i, j: (0, 0, 0)),
            pl.BlockSpec((2 * D_GROUP, D_GROUP), lambda i, j: (0, 0)),
        ],
        out_specs=blk,
        out_shape=jax.ShapeDtypeStruct(u.shape, F32),
        scratch_shapes=[
            pltpu.VMEM((n1c, DFT_BLOCK, D_GROUP), F32),
            pltpu.VMEM((n1c, DFT_BLOCK, D_GROUP), F32),
            pltpu.VMEM((n, 2 * D_GROUP), F32),
        ],
        compiler_params=_cparams(("arbitrary", "arbitrary")),
        name="fourier",
    )(u, jnp.asarray(cs), jnp.asarray(twc), jnp.asarray(tws), jnp.asarray(cc))


SEGMENTS = SUBLANES
PAD_ROWS = 8


def _expm1(x):
    u = jnp.exp(x)
    near = (u - 1.0) * x / jnp.log(jnp.where(u == 1.0, 2.0, u))
    return jnp.where(x < -0.5, u - 1.0, jnp.where(u == 1.0, x, near))


def _lru_kernel(has_h0, *refs):
    if has_h0:
        (ux_ref, ug_ref, cw_ref, cb_ref, wg_ref, bg_ref, lam_ref, h0_ref,
         y_ref, fin_ref, upad_ref, a_ref, b_ref, h_ref) = refs
    else:
        (ux_ref, ug_ref, cw_ref, cb_ref, wg_ref, bg_ref, lam_ref,
         y_ref, fin_ref, upad_ref, a_ref, b_ref, h_ref) = refs
    n = ux_ref.shape[0]
    seg_len = n // SEGMENTS
    tile = min(256, seg_len)

    zeros_pad = jnp.zeros((PAD_ROWS, LANES), F32)
    upad_ref[pl.ds(0, PAD_ROWS), :] = zeros_pad
    upad_ref[pl.ds(PAD_ROWS + n, PAD_ROWS), :] = zeros_pad
    upad_ref[pl.ds(PAD_ROWS, n), :] = ux_ref[...]

    lam = lam_ref[...]
    sp = jnp.maximum(-lam, 0.0) + jnp.log1p(jnp.exp(-jnp.abs(lam)))
    cw = cw_ref[...]
    cb = cb_ref[...]
    wg = wg_ref[...].astype(BF16)
    bg = bg_ref[...]
    left = (CONV_W - 1) // 2

    for t in range(n // tile):
        r0 = t * tile
        xc = cb
        for k in range(CONV_W):
            xc = xc + upad_ref[pl.ds(PAD_ROWS + r0 + k - left, tile), :] * cw[k:k + 1, :]
        gates = jnp.dot(xc.astype(BF16), wg, preferred_element_type=F32) + bg
        seg = r0 // seg_len
        s0 = r0 - seg * seg_len
        for d in range(2):
            r = jax.nn.sigmoid(gates[:, (2 * d) * LANES:(2 * d + 1) * LANES])
            gi = jax.nn.sigmoid(gates[:, (2 * d + 1) * LANES:(2 * d + 2) * LANES])
            log_a = (-LRU_C) * r * sp[d:d + 1, :]
            a = jnp.exp(log_a)
            drive = jnp.sqrt(-_expm1(2.0 * log_a)) * (gi * xc)
            dst = pl.ds(s0 * SEGMENTS + seg, tile, stride=SEGMENTS)
            a_ref[d, dst, :] = a
            b_ref[d, dst, :] = drive

    def rows(s):
        return pl.ds(pl.multiple_of(s * SEGMENTS, SEGMENTS), SEGMENTS)

    def pass1(t, carry):
        hf, af, hb, ab = carry
        sb = seg_len - 1 - t
        a0 = a_ref[0, rows(t), :]
        a1 = a_ref[1, rows(sb), :]
        hf = a0 * hf + b_ref[0, rows(t), :]
        hb = a1 * hb + b_ref[1, rows(sb), :]
        return hf, af * a0, hb, ab * a1

    z = jnp.zeros((SEGMENTS, LANES), F32)
    o = jnp.ones((SEGMENTS, LANES), F32)
    hf, af, hb, ab = lax.fori_loop(0, seg_len, pass1, (z, o, z, o), unroll=8)

    if has_h0:
        cf = h0_ref[0:1, :]
        cbk = h0_ref[1:2, :]
    else:
        cf = jnp.zeros((1, LANES), F32)
        cbk = jnp.zeros((1, LANES), F32)
    sub = lax.broadcasted_iota(jnp.int32, (SEGMENTS, LANES), 0)
    carry_f = z
    for j in range(SEGMENTS):
        carry_f = jnp.where(sub == j, cf, carry_f)
        cf = hf[j:j + 1, :] + af[j:j + 1, :] * cf
    carry_b = z
    for j in reversed(range(SEGMENTS)):
        carry_b = jnp.where(sub == j, cbk, carry_b)
        cbk = hb[j:j + 1, :] + ab[j:j + 1, :] * cbk
    fin_ref[0:1, :] = cf
    fin_ref[1:2, :] = cbk

    def pass2(t, carry):
        hf, hb = carry
        sb = seg_len - 1 - t
        hf = a_ref[0, rows(t), :] * hf + b_ref[0, rows(t), :]
        h_ref[0, rows(t), :] = hf
        hb = a_ref[1, rows(sb), :] * hb + b_ref[1, rows(sb), :]
        h_ref[1, rows(sb), :] = hb
        return hf, hb

    lax.fori_loop(0, seg_len, pass2, (carry_f, carry_b), unroll=8)

    for t in range(n // tile):
        r0 = t * tile
        seg = r0 // seg_len
        s0 = r0 - seg * seg_len
        src = pl.ds(s0 * SEGMENTS + seg, tile, stride=SEGMENTS)
        hsum = h_ref[0, src, :] + h_ref[1, src, :]
        y_ref[pl.ds(r0, tile), :] = hsum * jax.nn.gelu(ug_ref[pl.ds(r0, tile), :])


def _lru_call(ux, ug, lp, h0):
    b, n, _ = ux.shape
    nblk = D_LRU // LANES
    has_h0 = h0 is not None
    blk = pl.BlockSpec((None, n, LANES), lambda i, j: (i, 0, j))
    in_specs = [
        blk, blk,
        pl.BlockSpec((CONV_W, LANES), lambda i, j: (0, j)),
        pl.BlockSpec((1, LANES), lambda i, j: (0, j)),
        pl.BlockSpec((None, LANES, 4 * LANES), lambda i, j: (j, 0, 0)),
        pl.BlockSpec((None, 1, 4 * LANES), lambda i, j: (j, 0, 0)),
        pl.BlockSpec((2, LANES), lambda i, j: (0, j)),
    ]
    args = [ux, ug, lp["conv_w"], lp["conv_b"], lp["w_gates"], lp["b_gates"], lp["lam"]]
    if has_h0:
        in_specs.append(pl.BlockSpec((None, 2, LANES), lambda i, j: (i, 0, j)))
        args.append(h0)
    return pl.pallas_call(
        functools.partial(_lru_kernel, has_h0),
        grid=(b, nblk),
        in_specs=in_specs,
        out_specs=[blk, pl.BlockSpec((None, 2, LANES), lambda i, j: (i, 0, j))],
        out_shape=[jax.ShapeDtypeStruct((b, n, D_LRU), F32),
                   jax.ShapeDtypeStruct((b, 2, D_LRU), F32)],
        scratch_shapes=[
            pltpu.VMEM((n + 2 * PAD_ROWS, LANES), F32),
            pltpu.VMEM((2, n, LANES), F32),
            pltpu.VMEM((2, n, LANES), F32),
            pltpu.VMEM((2, n, LANES), F32),
        ],
        compiler_params=_cparams(("arbitrary", "arbitrary")),
        name="lru",
    )(*args)


def _lru_params(conv_w, conv_b, wa, ba, wi, bi, lam):
    nblk = D_LRU // LANES
    hpb = LANES // LRU_HEAD_DIM

    def blockdiag(w):
        w = w.reshape(nblk, hpb, LRU_HEAD_DIM, LRU_HEAD_DIM)
        eye = jnp.eye(hpb, dtype=w.dtype)
        return jnp.einsum("bhij,hg->bhigj", w, eye).reshape(nblk, LANES, LANES)

    w_gates = jnp.concatenate([blockdiag(wa[0]), blockdiag(wi[0]), blockdiag(wa[1]), blockdiag(wi[1])], axis=-1)
    b_gates = jnp.stack([ba[0], bi[0], ba[1], bi[1]], axis=0)
    b_gates = b_gates.reshape(4, nblk, LANES).transpose(1, 0, 2).reshape(nblk, 1, 4 * LANES)
    return {"conv_w": conv_w, "conv_b": conv_b.reshape(1, D_LRU), "w_gates": w_gates,
            "b_gates": b_gates, "lam": lam}


def _mixout_kernel(x_ref, yf_ref, yr_ref, w_ref, g1_ref, n2_ref, sh_ref, sc_ref, wr_ref,
                   xo_ref, h2_ref, lg_ref, wbf_ref):
    @pl.when((pl.program_id(0) == 0) & (pl.program_id(1) == 0))
    def _cast_w():
        wbf_ref[...] = w_ref[...].astype(BF16)

    mix = jnp.dot(yf_ref[...].astype(BF16), wbf_ref[:D_FOURIER, :], preferred_element_type=F32)
    mix = mix + jnp.dot(yr_ref[...].astype(BF16), wbf_ref[D_FOURIER:, :], preferred_element_type=F32)
    x = x_ref[...] + g1_ref[...] * mix
    xo_ref[...] = x
    h2 = _rms_mod(x, n2_ref[...], sh_ref[...], sc_ref[...])
    tm = h2.shape[0]
    for s in range(TOKEN_ROWS):
        h2_ref[pl.ds(s, tm, stride=TOKEN_ROWS), :] = h2[:, s * LANES:(s + 1) * LANES]
    lg_ref[...] = lax.dot_general(wr_ref[...], h2, (((1,), (1,)), ((), ())),
                                  preferred_element_type=F32, precision=lax.Precision.HIGHEST)


def _mixout_call(x, yf, yr, w_out, g1, n2g, sh2, sc2, w_router_t, layer=0):
    b, n, d = x.shape
    tm = min(512, n)
    row = pl.BlockSpec((None, tm, d), lambda i, j: (i, j, 0))
    half = pl.BlockSpec((None, tm, D_FOURIER), lambda i, j: (i, j, 0))
    vec_b = pl.BlockSpec((None, 1, d), lambda i, j: (i, 0, 0))
    return pl.pallas_call(
        _mixout_kernel,
        grid=(b, n // tm),
        in_specs=[row, half, half,
                  _layer_weight_spec(w_out, layer),
                  vec_b,
                  pl.BlockSpec((1, d), lambda i, j: (0, 0)),
                  vec_b, vec_b,
                  pl.BlockSpec((N_EXPERTS, d), lambda i, j: (0, 0))],
        out_specs=[row,
                   pl.BlockSpec((tm * TOKEN_ROWS, LANES), lambda i, j: (i * (n // tm) + j, 0)),
                   pl.BlockSpec((None, N_EXPERTS, tm), lambda i, j: (i, 0, j))],
        out_shape=[jax.ShapeDtypeStruct((b, n, d), F32),
                   jax.ShapeDtypeStruct((b * n * TOKEN_ROWS, LANES), F32),
                   jax.ShapeDtypeStruct((b, N_EXPERTS, n), F32)],
        scratch_shapes=[pltpu.VMEM((d, d), BF16)],
        compiler_params=_cparams(("arbitrary", "arbitrary")),
        name="mixout",
    )(x, yf, yr, w_out, g1, n2g.reshape(1, d), sh2, sc2, w_router_t)


def _lane_prefix_sum(x):
    n = x.shape[-1]
    lane = lax.broadcasted_iota(jnp.int32, x.shape, x.ndim - 1)
    s = 1
    while s < n:
        x = x + jnp.where(lane >= s, pltpu.roll(x, s, x.ndim - 1), 0)
        s *= 2
    return x


def _route_kernel(cap, lg_ref, key_ref, csum_ref, keyt_ref, afft_ref):
    n = lg_ref.shape[1]
    lg = lg_ref[...]
    m = jnp.max(lg, axis=0, keepdims=True)
    ex = jnp.exp(lg - m)
    aff = ex / jnp.sum(ex, axis=0, keepdims=True)
    bits = pltpu.bitcast(aff, jnp.int32)

    def bis(_, carry):
        lo, hi = carry
        mid = lo + ((hi - lo + 1) >> 1)
        cnt = jnp.sum((bits >= mid).astype(jnp.int32), axis=1, keepdims=True)
        ok = cnt >= cap
        return jnp.where(ok, mid, lo), jnp.where(ok, hi, mid - 1)

    lo0 = jnp.zeros((N_EXPERTS, 1), jnp.int32)
    hi0 = jnp.full((N_EXPERTS, 1), 0x7F800000, jnp.int32)
    thr, _ = lax.fori_loop(0, 32, bis, (lo0, hi0))

    thr_f = pltpu.bitcast(thr, F32)
    window = 2.0 ** -6
    flo0 = thr_f * (1.0 - window)
    fhi0 = jnp.maximum(thr_f * (1.0 + window), jnp.float32(1e-37))

    def fbis(_, carry):
        lo, hi = carry
        mid = lo + 0.5 * (hi - lo)
        cnt = jnp.sum((aff >= mid).astype(jnp.int32), axis=1, keepdims=True)
        ok = cnt >= cap
        return jnp.where(ok, mid, lo), jnp.where(ok, hi, mid)

    flo, fhi = lax.fori_loop(0, 32, fbis, (flo0, fhi0))
    gt = aff >= fhi
    eq = (aff >= flo) & (aff < fhi)
    need = cap - jnp.sum(gt.astype(jnp.int32), axis=1, keepdims=True)
    eq_rank = _lane_prefix_sum(eq.astype(jnp.int32)) - eq.astype(jnp.int32)
    sel = gt | (eq & (eq_rank < need))
    csum = _lane_prefix_sum(sel.astype(jnp.int32))
    key = jnp.where(sel, csum - 1, -1)
    key_ref[...] = key
    csum_ref[...] = csum

    pad = jnp.zeros((LANES - N_EXPERTS, n), F32)
    keyt_ref[...] = jnp.concatenate([key.astype(F32), pad - 1.0], axis=0).T
    afft_ref[...] = jnp.concatenate([aff, pad], axis=0).T


def _route_call(logits_t, cap):
    b, _, n = logits_t.shape
    en = pl.BlockSpec((None, N_EXPERTS, n), lambda i: (i, 0, 0))
    tm = pl.BlockSpec((None, n, LANES), lambda i: (i, 0, 0))
    return pl.pallas_call(
        functools.partial(_route_kernel, cap),
        grid=(b,),
        in_specs=[en],
        out_specs=[en, en, tm, tm],
        out_shape=[jax.ShapeDtypeStruct((b, N_EXPERTS, n), jnp.int32),
                   jax.ShapeDtypeStruct((b, N_EXPERTS, n), jnp.int32),
                   jax.ShapeDtypeStruct((b, n, LANES), F32),
                   jax.ShapeDtypeStruct((b, n, LANES), F32)],
        compiler_params=_cparams(("arbitrary",)),
        name="route",
    )(logits_t)


COMPACT_TILE = 64


def _compact_kernel(cap, klo_ref, khi_ref, key_ref, idx_ref):
    b = pl.program_id(0)
    nblk = key_ref.shape[1]
    rt = min(COMPACT_TILE, cap)
    ntile = cap // rt
    width = min(LANES, cap)
    per_store = width // rt
    lane = lax.broadcasted_iota(jnp.int32, (1, LANES), 1)
    sub = lax.broadcasted_iota(jnp.int32, (rt, 1), 0)
    base = b * (nblk * LANES)
    zero = jnp.zeros((rt, LANES), F32)
    for e in range(N_EXPERTS):
        row = jnp.zeros((1, LANES), F32)
        for t in range(ntile):
            s = (b * N_EXPERTS + e) * ntile + t
            rr = t * rt + sub

            def block(k, ids, e=e, rr=rr):
                p = key_ref[e, pl.ds(k, 1), :] == rr
                return ids + jnp.where(p, (base + k * LANES + lane).astype(F32), 0.0)

            ids = lax.fori_loop(klo_ref[s], khi_ref[s], block, zero)
            col = jnp.sum(ids, axis=1, keepdims=True)
            diag = (sub + (t % per_store) * rt) == lane
            row = row + jnp.sum(jnp.where(diag, col, 0.0), axis=0, keepdims=True)
            if (t + 1) % per_store == 0:
                g = t // per_store
                idx_ref[e:e + 1, g * width:(g + 1) * width] = row[:, :width].astype(jnp.int32)
                row = jnp.zeros((1, LANES), F32)


def _compact_call(key, csum, cap):
    b, _, n = key.shape
    nblk = n // LANES
    rt = min(COMPACT_TILE, cap)
    ntile = cap // rt
    cb_incl = csum[:, :, LANES - 1::LANES]
    cb_excl = jnp.concatenate([jnp.zeros_like(cb_incl[..., :1]), cb_incl[..., :-1]], axis=-1)
    t0 = (jnp.arange(ntile, dtype=jnp.int32) * rt)[:, None]
    klo = jnp.sum((cb_incl[:, :, None, :] <= t0).astype(jnp.int32), axis=-1)
    khi = jnp.sum((cb_excl[:, :, None, :] < t0 + rt).astype(jnp.int32), axis=-1)
    return pl.pallas_call(
        functools.partial(_compact_kernel, cap),
        grid_spec=pltpu.PrefetchScalarGridSpec(
            num_scalar_prefetch=2, grid=(b,),
            in_specs=[pl.BlockSpec((None, N_EXPERTS, nblk, LANES), lambda i, *_: (i, 0, 0, 0))],
            out_specs=pl.BlockSpec((None, N_EXPERTS, cap), lambda i, *_: (i, 0, 0))),
        out_shape=jax.ShapeDtypeStruct((b, N_EXPERTS, cap), jnp.int32),
        compiler_params=_cparams(("arbitrary",)),
        name="compact",
    )(klo.reshape(-1), khi.reshape(-1), key.reshape(b, N_EXPERTS, nblk, LANES))


FF_TILE = 256


FF_STEPS = D_FF // FF_TILE
ROW_ALIGN = 32


def _round_up(v, m):
    return -(-v // m) * m


def _ffn_layout(row_counts):
    layout, off = [], 0
    for m in row_counts:
        padded = _round_up(m, FF_STEPS * SUBLANES)
        layout.append((m, padded, off))
        off = _round_up(off + padded, ROW_ALIGN)
    out_rows = _round_up(layout[-1][2] + layout[-1][0], ROW_ALIGN)
    scratch_rows = _round_up(layout[-1][2] + layout[-1][1], SUBLANES)
    return tuple(layout), out_rows, max(scratch_rows, out_rows)


def _ffn_kernel(layout, *refs):
    ns = len(layout)
    idx_refs = refs[:ns]
    tabs = refs[ns:2 * ns]
    wg_ref, wu_ref, wd_ref, o_ref, xg32_ref, xg_ref, acc_ref, sem = refs[2 * ns:]
    e = pl.program_id(0)
    f = pl.program_id(1)
    n_e = pl.num_programs(0)
    slot = e % 2

    def issue_chunk(expert, dst_slot, step):
        for s, (_, padded, off) in enumerate(layout):
            chunk = padded // FF_STEPS
            src0 = expert * padded + step * chunk
            dst0 = off + step * chunk

            def body(g, carry, s=s, src0=src0, dst0=dst0):
                src = src0 + g * SUBLANES
                dst = pl.multiple_of(dst0 + g * SUBLANES, SUBLANES)
                for j in range(SUBLANES):
                    tok = idx_refs[s][src + j]
                    pltpu.make_async_copy(
                        tabs[s].at[pl.ds(pl.multiple_of(tok * TOKEN_ROWS, TOKEN_ROWS), TOKEN_ROWS), :],
                        xg32_ref.at[dst_slot, pl.ds(pl.multiple_of((dst + j) * TOKEN_ROWS, TOKEN_ROWS), TOKEN_ROWS), :],
                        sem.at[dst_slot]).start()
                return carry
            lax.fori_loop(0, chunk // SUBLANES, body, 0)

    @pl.when((e == 0) & (f == 0))
    def _first_expert():
        for step in range(FF_STEPS):
            issue_chunk(0, 0, step)

    @pl.when(f == 0)
    def _start_expert():
        for s, (m, padded, off) in enumerate(layout):
            pltpu.make_async_copy(tabs[s].at[pl.ds(0, padded * TOKEN_ROWS), :],
                                  xg32_ref.at[slot, pl.ds(off * TOKEN_ROWS, padded * TOKEN_ROWS), :],
                                  sem.at[slot]).wait()
        pack = 2 * SUBLANES
        for m, _, off in layout:
            def unpack(g, carry, off=off):
                t0 = pl.multiple_of(off + g * pack, pack)
                for c in range(TOKEN_ROWS):
                    v = xg32_ref[slot, pl.ds(t0 * TOKEN_ROWS + c, pack, stride=TOKEN_ROWS), :]
                    xg_ref[pl.ds(t0, pack), c * LANES:(c + 1) * LANES] = v.astype(BF16)
                return carry
            lax.fori_loop(0, m // pack, unpack, 0)
            acc_ref[pl.ds(off, m), :] = jnp.zeros((m, acc_ref.shape[1]), F32)

    nxt = jnp.minimum(e + 1, n_e - 1)
    other = 1 - slot
    copies = []
    for s, (_, padded, off) in enumerate(layout):
        chunk = padded // FF_STEPS
        for r in range(chunk):
            copies.append((s, nxt * padded + f * chunk + r, off + f * chunk, r))
    tiles = [(off + i * min(512, m), min(512, m)) for m, _, off in layout for i in range(m // min(512, m))]
    share = -(-len(copies) // len(tiles))

    wg = wg_ref[...].astype(BF16)
    wu = wu_ref[...].astype(BF16)
    wd = wd_ref[...].astype(BF16)
    for ti, (r0, mt) in enumerate(tiles):
        for s, src, dst0, r in copies[ti * share:(ti + 1) * share]:
            tok = idx_refs[s][src]
            dst = pl.multiple_of(dst0 * TOKEN_ROWS, SUBLANES * TOKEN_ROWS) + r * TOKEN_ROWS
            pltpu.make_async_copy(
                tabs[s].at[pl.ds(pl.multiple_of(tok * TOKEN_ROWS, TOKEN_ROWS), TOKEN_ROWS), :],
                xg32_ref.at[other, pl.ds(pl.multiple_of(dst, TOKEN_ROWS), TOKEN_ROWS), :],
                sem.at[other]).start()
        rows = pl.ds(r0, mt)
        xg = xg_ref[rows, :]
        g = jnp.dot(xg, wg, preferred_element_type=F32)
        u = jnp.dot(xg, wu, preferred_element_type=F32)
        hid = (g * jax.nn.sigmoid(g) * u).astype(BF16)
        acc_ref[rows, :] += jnp.dot(hid, wd, preferred_element_type=F32)

    @pl.when((e == n_e - 1) & (f == FF_STEPS - 1))
    def _drain():
        for s, (_, padded, off) in enumerate(layout):
            pltpu.make_async_copy(tabs[s].at[pl.ds(0, padded * TOKEN_ROWS), :],
                                  xg32_ref.at[other, pl.ds(off * TOKEN_ROWS, padded * TOKEN_ROWS), :],
                                  sem.at[other]).wait()

    @pl.when(f == FF_STEPS - 1)
    def _finish():
        end = 0
        for m, _, off in layout:
            if off > end:
                o_ref[pl.ds(end, off - end), :] = jnp.zeros((off - end, o_ref.shape[1]), o_ref.dtype)
            o_ref[pl.ds(off, m), :] = acc_ref[pl.ds(off, m), :].astype(o_ref.dtype)
            end = off + m
        if o_ref.shape[0] > end:
            o_ref[pl.ds(end, o_ref.shape[0] - end), :] = jnp.zeros((o_ref.shape[0] - end, o_ref.shape[1]), o_ref.dtype)


def _ffn_call(layer, row_counts, idxs, tables, w_gate, w_up, w_down):
    ns = len(tables)
    n_e = w_gate.shape[1]
    d = w_gate.shape[2]
    layout, out_rows, scratch_rows = _ffn_layout(row_counts)
    grid_spec = pltpu.PrefetchScalarGridSpec(
        num_scalar_prefetch=ns,
        grid=(n_e, FF_STEPS),
        in_specs=[pl.BlockSpec(memory_space=pl.ANY)] * ns
        + [pl.BlockSpec((None, None, d, FF_TILE), lambda e, f, *_: (layer, e, 0, f)),
           pl.BlockSpec((None, None, d, FF_TILE), lambda e, f, *_: (layer, e, 0, f)),
           pl.BlockSpec((None, None, FF_TILE, d), lambda e, f, *_: (layer, e, f, 0))],
        out_specs=pl.BlockSpec((None, out_rows, d), lambda e, f, *_: (e, 0, 0)),
        scratch_shapes=[pltpu.VMEM((2, scratch_rows * TOKEN_ROWS, LANES), F32), pltpu.VMEM((scratch_rows, d), BF16),
                        pltpu.VMEM((scratch_rows, d), F32), pltpu.SemaphoreType.DMA((2,))],
    )
    return pl.pallas_call(
        functools.partial(_ffn_kernel, layout),
        grid_spec=grid_spec,
        out_shape=jax.ShapeDtypeStruct((n_e, out_rows, d), BF16),
        compiler_params=_cparams(("arbitrary", "arbitrary")),
        name="ffn",
    )(*idxs, *tables, w_gate, w_up, w_down)


COMBINE_TOKENS = 512
COMBINE_WINDOW = 256
ROW_PACK = 16


def _combine_kernel(final, win, start_ref, spill_ref, *refs):
    if final:
        x_ref, kt_ref, at_ref, y_ref, g2_ref, fg_ref, o_ref, acc_ref = refs
    else:
        x_ref, kt_ref, at_ref, y_ref, g2_ref, o_ref, acc_ref = refs
    cap = y_ref.shape[1]
    windowed = win < cap
    step = (pl.program_id(0) * pl.num_programs(1) + pl.program_id(1)) * N_EXPERTS
    kt = kt_ref[...]
    at = at_ref[...]
    lane = lax.broadcasted_iota(jnp.int32, (1, win), 1).astype(F32)

    acc = jnp.zeros(x_ref.shape, F32)
    for e in range(N_EXPERTS):
        if windowed:
            s0 = pl.multiple_of(start_ref[step + e], ROW_PACK)
            q = (kt[:, e:e + 1] == lane + s0.astype(F32)).astype(BF16)
            part = jnp.dot(q, y_ref[e, pl.ds(s0, win), :], preferred_element_type=F32)
        else:
            q = (kt[:, e:e + 1] == lane).astype(BF16)
            part = jnp.dot(q, y_ref[e], preferred_element_type=F32)
        acc = acc + at[:, e:e + 1] * part

    if windowed:
        acc_ref[...] = acc
        for e in range(N_EXPERTS):
            @pl.when(spill_ref[step + e] != 0)
            def _rest(e=e):
                done = (start_ref[step + e] + win).astype(F32)
                ke = kt_ref[:, e:e + 1]
                ke = jnp.where(ke >= done, ke, -1.0)
                q = (ke == lane + float(cap - win)).astype(BF16)
                acc_ref[...] += at_ref[:, e:e + 1] * jnp.dot(q, y_ref[e, cap - win:, :],
                                                             preferred_element_type=F32)
        acc = acc_ref[...]

    x = x_ref[...] + g2_ref[...] * acc
    if final:
        x = x * lax.rsqrt(jnp.mean(x * x, axis=-1, keepdims=True) + EPS) * fg_ref[...]
    o_ref[...] = x


def _combine_call(x, keyt, afft, csum, yexp, first_row, cap, g2, final_g):
    b, n, d = x.shape
    tc = min(COMBINE_TOKENS, n)
    win = min(COMBINE_WINDOW, cap)
    assert cap in (win, 2 * win)
    final = final_g is not None
    blk0 = first_row // cap
    upto = csum[:, :, tc - 1::tc]
    before = jnp.concatenate([jnp.zeros_like(upto[..., :1]), upto[..., :-1]], axis=-1)
    start = jnp.clip((before // ROW_PACK) * ROW_PACK, 0, cap - win)
    spill = (upto > start + win).astype(jnp.int32)
    start = start.transpose(0, 2, 1).reshape(-1)
    spill = spill.transpose(0, 2, 1).reshape(-1)
    row = pl.BlockSpec((None, tc, d), lambda i, j, *_: (i, j, 0))
    tok = pl.BlockSpec((None, tc, LANES), lambda i, j, *_: (i, j, 0))
    in_specs = [row, tok, tok,
                pl.BlockSpec((N_EXPERTS, cap, d), lambda i, j, *_: (0, blk0 + i, 0)),
                pl.BlockSpec((None, 1, d), lambda i, j, *_: (i, 0, 0))]
    args = [x, keyt, afft, yexp, g2]
    if final:
        in_specs.append(pl.BlockSpec((1, d), lambda i, j, *_: (0, 0)))
        args.append(final_g.reshape(1, d))
    return pl.pallas_call(
        functools.partial(_combine_kernel, final, win),
        grid_spec=pltpu.PrefetchScalarGridSpec(
            num_scalar_prefetch=2, grid=(b, n // tc), in_specs=in_specs, out_specs=row,
            scratch_shapes=[pltpu.VMEM((tc, d), F32)]),
        out_shape=jax.ShapeDtypeStruct((b, n, d), F32),
        compiler_params=_cparams(("arbitrary", "arbitrary")),
        name="combine",
    )(start, spill, *args)


def _grid_pos_embed(rows, d):
    rr, cc = np.meshgrid(np.arange(rows, dtype=np.float64), np.arange(GRID_W, dtype=np.float64), indexing="ij")
    quarter = d // 4
    omega = 1.0 / (POS_BASE ** (np.arange(quarter, dtype=np.float64) / quarter))

    def enc(p):
        ang = p.reshape(-1)[:, None] * omega
        return np.concatenate([np.sin(ang), np.cos(ang)], axis=-1)

    return np.concatenate([enc(rr), enc(cc)], axis=-1).astype(np.float32)


def _moe(layer, streams, w_gate, w_up, w_down):
    d = streams[0]["x"].shape[-1]
    caps = [EC_CAPACITY * s["x"].shape[1] // N_EXPERTS for s in streams]
    row_counts = [s["x"].shape[0] * cap for s, cap in zip(streams, caps)]
    layout, _, _ = _ffn_layout(row_counts)
    idxs, routes = [], []
    for s, cap, (m, padded, off) in zip(streams, caps, layout):
        key, csum, keyt, afft = _route_call(s["logits_t"], cap)
        idx = _compact_call(key, csum, cap)
        idx = idx.transpose(1, 0, 2).reshape(N_EXPERTS, m)
        idxs.append(jnp.pad(idx, ((0, 0), (0, padded - m))).reshape(N_EXPERTS * padded))
        routes.append((keyt, afft, csum))
    tables = [s["h2"] for s in streams]
    yexp = _ffn_call(layer, row_counts, idxs, tables, w_gate, w_up, w_down)
    return [_combine_call(s["x"], keyt, afft, csum, yexp, off, cap, s["g2"], s["final_g"])
            for s, (keyt, afft, csum), cap, (_, _, off) in zip(streams, routes, caps, layout)]


def kernel(x, c, ctx, c_ctx, w_mod, b_mod, norm1_g, norm2_g, w_in, conv_w, conv_b, lru_wa, lru_ba,
           lru_wi, lru_bi, lru_lambda, w_out, w_router, w_gate, w_up, w_down, final_g):
    bsz, n, d = x.shape
    depth = w_mod.shape[0]
    pos = jnp.asarray(_grid_pos_embed(n // GRID_W, d))

    crows = jnp.zeros((SUBLANES, d), F32).at[:bsz].set(c).at[bsz].set(c_ctx)
    mod = _mod_call(crows, w_mod, b_mod)

    for l in range(depth):
        last = l == depth - 1
        mx = mod[l, :bsz].reshape(bsz, 1, N_MOD, d)
        sh1, sc1, g1, sh2, sc2, g2 = [mx[:, :, i, :] for i in range(N_MOD)]
        mc = jnp.broadcast_to(mod[l, bsz].reshape(1, 1, N_MOD, d), (bsz, 1, N_MOD, d))
        csh1, csc1, cg1, csh2, csc2, cg2 = [mc[:, :, i, :] for i in range(N_MOD)]
        w_router_t = w_router[l].T
        lp = _lru_params(conv_w[l], conv_b[l], lru_wa[l], lru_ba[l], lru_wi[l], lru_bi[l], lru_lambda[l])

        cuf, cux, cug = _proj_call(ctx, None, norm1_g[l], csh1, csc1, w_in, l)
        cyr, ctx_states = _lru_call(cux, cug, lp, None)
        streams = []
        if not last:
            cyf = _fourier_call(cuf)
            ctx, ch2, clg = _mixout_call(ctx, cyf, cyr, w_out, cg1, norm2_g[l], csh2, csc2, w_router_t, l)
            streams.append(dict(x=ctx, h2=ch2, logits_t=clg, g2=cg2, final_g=None))

        if l == 0:
            uf, ux, ug, x = _proj_call(x, pos, norm1_g[l], sh1, sc1, w_in, l)
        else:
            uf, ux, ug = _proj_call(x, None, norm1_g[l], sh1, sc1, w_in, l)
        yf = _fourier_call(uf)
        yr, _ = _lru_call(ux, ug, lp, ctx_states)
        x, h2, lg = _mixout_call(x, yf, yr, w_out, g1, norm2_g[l], sh2, sc2, w_router_t, l)
        streams.insert(0, dict(x=x, h2=h2, logits_t=lg, g2=g2, final_g=final_g if last else None))
        outs = _moe(l, streams, w_gate, w_up, w_down)
        x = outs[0]
        if not last:
            ctx = outs[1]
    return x
```

```python
import functools
import math

import numpy as np
import jax
import jax.numpy as jnp
from jax import lax
from jax.experimental import pallas as pl
from jax.experimental.pallas import tpu as pltpu

D_MODEL = 1024
GRID_W = 64
POS_BASE = 10000.0
D_FOURIER = 512
FOURIER_GROUPS = 4
D_GROUP = D_FOURIER // FOURIER_GROUPS
D_LRU = 512
LRU_HEADS = 8
LRU_HEAD_DIM = D_LRU // LRU_HEADS
CONV_W = 4
LRU_C = 8.0
D_IN = D_FOURIER + 2 * D_LRU
N_EXPERTS = 16
EC_CAPACITY = 2
D_FF = 2816
N_MOD = 6
EPS = 1e-6

LANES = 128
SUBLANES = 8
DFT_BLOCK = 256
TOKEN_ROWS = D_MODEL // LANES
ROW_TILE = 512
VMEM_LIMIT = 56 * 1024 * 1024

F32 = jnp.float32
BF16 = jnp.bfloat16


def _cparams(sem, **kw):
    return pltpu.CompilerParams(dimension_semantics=sem, vmem_limit_bytes=VMEM_LIMIT, **kw)


def _mod_kernel(c_ref, w_ref, b_ref, o_ref):
    c = c_ref[...]
    cond = c * jax.nn.sigmoid(c)
    o_ref[...] = jnp.dot(cond, w_ref[...], preferred_element_type=F32,
                         precision=lax.Precision.HIGHEST) + b_ref[...]


def _mod_call(crows, w_mod, b_mod):
    depth, d, nm = w_mod.shape
    tn = 1536
    return pl.pallas_call(
        _mod_kernel,
        grid=(depth, nm // tn),
        in_specs=[
            pl.BlockSpec((SUBLANES, d), lambda l, j: (0, 0)),
            pl.BlockSpec((None, d, tn), lambda l, j: (l, 0, j)),
            pl.BlockSpec((None, 1, tn), lambda l, j: (l, 0, j)),
        ],
        out_specs=pl.BlockSpec((None, SUBLANES, tn), lambda l, j: (l, 0, j)),
        out_shape=jax.ShapeDtypeStruct((depth, SUBLANES, nm), F32),
        compiler_params=_cparams(("arbitrary", "arbitrary")),
        name="mod",
    )(crows, w_mod, b_mod.reshape(depth, 1, nm))


def _rms_mod(x, g, shift, scale):
    y = x * lax.rsqrt(jnp.mean(x * x, axis=-1, keepdims=True) + EPS) * g
    return y * (1.0 + scale) + shift


def _proj_kernel(add_pos, *refs):
    if add_pos:
        x_ref, pos_ref, g_ref, sh_ref, sc_ref, w_ref, uf_ref, ux_ref, ug_ref, xp_ref, wbf_ref = refs
        x = x_ref[...] + pos_ref[...]
        xp_ref[...] = x
    else:
        x_ref, g_ref, sh_ref, sc_ref, w_ref, uf_ref, ux_ref, ug_ref, wbf_ref = refs
        x = x_ref[...]

    @pl.when((pl.program_id(0) == 0) & (pl.program_id(1) == 0))
    def _cast_w():
        wbf_ref[...] = w_ref[...].astype(BF16)

    h = _rms_mod(x, g_ref[...], sh_ref[...], sc_ref[...])
    u = jnp.dot(h.astype(BF16), wbf_ref[...], preferred_element_type=F32)
    uf_ref[...] = u[:, :D_FOURIER]
    ux_ref[...] = u[:, D_FOURIER:D_FOURIER + D_LRU]
    ug_ref[...] = u[:, D_FOURIER + D_LRU:]


def _layer_weight_spec(w, layer):
    rows, cols = w.shape[-2:]
    if w.ndim == 2:
        return pl.BlockSpec((rows, cols), lambda i, j: (0, 0))
    return pl.BlockSpec((None, rows, cols), lambda i, j: (layer, 0, 0))


def _proj_call(x, pos, g, shift, scale, w_in, layer=0):
    b, n, d = x.shape
    tm = min(ROW_TILE, n)
    add_pos = pos is not None
    row = pl.BlockSpec((None, tm, d), lambda i, j: (i, j, 0))
    vec_b = pl.BlockSpec((None, 1, d), lambda i, j: (i, 0, 0))
    in_specs = [row]
    args = [x]
    if add_pos:
        in_specs.append(pl.BlockSpec((tm, d), lambda i, j: (j, 0)))
        args.append(pos)
    in_specs += [pl.BlockSpec((1, d), lambda i, j: (0, 0)), vec_b, vec_b,
                 _layer_weight_spec(w_in, layer)]
    args += [g.reshape(1, d), shift, scale, w_in]
    part = pl.BlockSpec((None, tm, D_FOURIER), lambda i, j: (i, j, 0))
    out_specs = [part, part, part]
    out_shape = [jax.ShapeDtypeStruct((b, n, D_FOURIER), F32)] * 3
    if add_pos:
        out_specs.append(row)
        out_shape.append(jax.ShapeDtypeStruct((b, n, d), F32))
    return pl.pallas_call(
        functools.partial(_proj_kernel, add_pos),
        grid=(b, n // tm),
        in_specs=in_specs,
        out_specs=out_specs,
        out_shape=out_shape,
        scratch_shapes=[pltpu.VMEM((d, D_IN), BF16)],
        compiler_params=_cparams(("arbitrary", "arbitrary")),
        name="proj",
    )(*args)


def _cmul_const(re, im, c, s):
    tol = 1e-12
    if abs(s) < tol:
        if abs(c - 1.0) < tol:
            return re, im
        if abs(c + 1.0) < tol:
            return -re, -im
        return re * c, im * c
    if abs(c) < tol:
        if abs(s - 1.0) < tol:
            return -im, re
        if abs(s + 1.0) < tol:
            return im, -re
        return -im * s, re * s
    return re * c - im * s, re * s + im * c


def _fft_list(xs):
    n = len(xs)
    if n == 1:
        return xs
    ev = _fft_list(xs[0::2])
    od = _fft_list(xs[1::2])
    out = [None] * n
    for k in range(n // 2):
        ang = -2.0 * math.pi * k / n
        tr, ti = _cmul_const(od[k][0], od[k][1], math.cos(ang), math.sin(ang))
        out[k] = (ev[k][0] + tr, ev[k][1] + ti)
        out[k + n // 2] = (ev[k][0] - tr, ev[k][1] - ti)
    return out


def _fourier_kernel(n1_count, u_ref, cs_ref, twc_ref, tws_ref, cc_ref, o_ref, tr_ref, ti_ref, pq_ref):
    n = u_ref.shape[0]
    cs = cs_ref[...].astype(BF16)
    for n1 in range(n1_count):
        if n1_count == 1:
            z = u_ref[...]
        else:
            z = u_ref[pl.ds(n1, DFT_BLOCK, stride=n1_count), :]
        g = jnp.dot(cs, z.astype(BF16), preferred_element_type=F32)
        gr = g[:DFT_BLOCK]
        gs = g[DFT_BLOCK:]
        if n1 == 0:
            tr_ref[n1] = gr
            ti_ref[n1] = -gs
        else:
            c = twc_ref[n1]
            s = tws_ref[n1]
            tr_ref[n1] = gr * c - gs * s
            ti_ref[n1] = -(gr * s + gs * c)

    if n1_count == 1:
        pq_ref[:, :D_GROUP] = tr_ref[0]
        pq_ref[:, D_GROUP:] = -ti_ref[0]
    else:
        def chunk(j, carry):
            r0 = pl.multiple_of(j * SUBLANES, SUBLANES)
            xs = [(tr_ref[i, pl.ds(r0, SUBLANES), :], ti_ref[i, pl.ds(r0, SUBLANES), :])
                  for i in range(n1_count)]
            ys = _fft_list(xs)
            for k1 in range(n1_count):
                rows = pl.ds(pl.multiple_of(k1 * DFT_BLOCK + r0, SUBLANES), SUBLANES)
                pq_ref[rows, :D_GROUP] = ys[k1][0]
                pq_ref[rows, D_GROUP:] = -ys[k1][1]
            return carry
        lax.fori_loop(0, DFT_BLOCK // SUBLANES, chunk, 0)

    cc = cc_ref[...].astype(BF16)
    tile = min(ROW_TILE, n)
    for i in range(n // tile):
        rows = pl.ds(i * tile, tile)
        o_ref[rows, :] = jnp.dot(pq_ref[rows, :].astype(BF16), cc, preferred_element_type=F32)


@functools.lru_cache(maxsize=None)
def _fourier_consts(n):
    n1c = n // DFT_BLOCK
    k = np.arange(DFT_BLOCK, dtype=np.float64)
    ang = 2.0 * np.pi * np.outer(k, k) / DFT_BLOCK
    cs = np.concatenate([np.cos(ang), np.sin(ang)], axis=0)
    tw = 2.0 * np.pi * np.outer(np.arange(n1c, dtype=np.float64), k) / n
    twc = np.repeat(np.cos(tw)[:, :, None], LANES, axis=2)
    tws = np.repeat(np.sin(tw)[:, :, None], LANES, axis=2)
    c = np.arange(D_GROUP, dtype=np.float64)
    angc = 2.0 * np.pi * np.outer(c, c) / D_GROUP
    scale = 1.0 / math.sqrt(n * D_GROUP)
    cc = np.concatenate([np.cos(angc), -np.sin(angc)], axis=0) * scale
    return (cs.astype(np.float32), twc.astype(np.float32), tws.astype(np.float32), cc.astype(np.float32))


def _fourier_call(u):
    b, n, _ = u.shape
    n1c = n // DFT_BLOCK
    cs, twc, tws, cc = _fourier_consts(n)
    blk = pl.BlockSpec((None, n, D_GROUP), lambda i, j: (i, 0, j))
    return pl.pallas_call(
        functools.partial(_fourier_kernel, n1c),
        grid=(b, FOURIER_GROUPS),
        in_specs=[
            blk,
            pl.BlockSpec((2 * DFT_BLOCK, DFT_BLOCK), lambda i, j: (0, 0)),
            pl.BlockSpec((n1c, DFT_BLOCK, LANES), lambda i, j: (0, 0, 0)),
            pl.BlockSpec((n1c, DFT_BLOCK, LANES), lambda i, j: (0, 0, 0)),
            pl.BlockSpec((2 * D_GROUP, D_GROUP), lambda i, j: (0, 0)),
        ],
        out_specs=blk,
        out_shape=jax.ShapeDtypeStruct(u.shape, F32),
        scratch_shapes=[
            pltpu.VMEM((n1c, DFT_BLOCK, D_GROUP), F32),
            pltpu.VMEM((n1c, DFT_BLOCK, D_GROUP), F32),
            pltpu.VMEM((n, 2 * D_GROUP), F32),
        ],
        compiler_params=_cparams(("arbitrary", "arbitrary")),
        name="fourier",
    )(u, jnp.asarray(cs), jnp.asarray(twc), jnp.asarray(tws), jnp.asarray(cc))


SEGMENTS = SUBLANES
PAD_ROWS = 8
SCAN_UNROLL = 16


def _expm1(x):
    u = jnp.exp(x)
    near = (u - 1.0) * x / jnp.log(jnp.where(u == 1.0, 2.0, u))
    return jnp.where(x < -0.5, u - 1.0, jnp.where(u == 1.0, x, near))


def _lru_kernel(has_h0, *refs):
    if has_h0:
        (ux_ref, ug_ref, cw_ref, cb_ref, wg_ref, bg_ref, lam_ref, h0_ref,
         y_ref, fin_ref, upad_ref, a_ref, b_ref, h_ref) = refs
    else:
        (ux_ref, ug_ref, cw_ref, cb_ref, wg_ref, bg_ref, lam_ref,
         y_ref, fin_ref, upad_ref, a_ref, b_ref, h_ref) = refs
    n = ux_ref.shape[0]
    seg_len = n // SEGMENTS
    tile = min(256, seg_len)

    zeros_pad = jnp.zeros((PAD_ROWS, LANES), F32)
    upad_ref[pl.ds(0, PAD_ROWS), :] = zeros_pad
    upad_ref[pl.ds(PAD_ROWS + n, PAD_ROWS), :] = zeros_pad
    upad_ref[pl.ds(PAD_ROWS, n), :] = ux_ref[...]

    lam = lam_ref[...]
    sp = jnp.maximum(-lam, 0.0) + jnp.log1p(jnp.exp(-jnp.abs(lam)))
    cw = cw_ref[...]
    cb = cb_ref[...]
    wg = wg_ref[...].astype(BF16)
    bg = bg_ref[...]
    left = (CONV_W - 1) // 2

    for t in range(n // tile):
        r0 = t * tile
        xc = cb
        for k in range(CONV_W):
            xc = xc + upad_ref[pl.ds(PAD_ROWS + r0 + k - left, tile), :] * cw[k:k + 1, :]
        gates = jnp.dot(xc.astype(BF16), wg, preferred_element_type=F32) + bg
        seg = r0 // seg_len
        s0 = r0 - seg * seg_len
        for d in range(2):
            r = jax.nn.sigmoid(gates[:, (2 * d) * LANES:(2 * d + 1) * LANES])
            gi = jax.nn.sigmoid(gates[:, (2 * d + 1) * LANES:(2 * d + 2) * LANES])
            log_a = (-LRU_C) * r * sp[d:d + 1, :]
            a = jnp.exp(log_a)
            drive = jnp.sqrt(-_expm1(2.0 * log_a)) * (gi * xc)
            dst = pl.ds(s0 * SEGMENTS + seg, tile, stride=SEGMENTS)
            a_ref[d, dst, :] = a
            b_ref[d, dst, :] = drive

    def rows(s):
        return pl.ds(pl.multiple_of(s * SEGMENTS, SEGMENTS), SEGMENTS)

    def pass1(t, carry):
        hf, af, hb, ab = carry
        sb = seg_len - 1 - t
        a0 = a_ref[0, rows(t), :]
        a1 = a_ref[1, rows(sb), :]
        hf = a0 * hf + b_ref[0, rows(t), :]
        hb = a1 * hb + b_ref[1, rows(sb), :]
        return hf, af * a0, hb, ab * a1

    z = jnp.zeros((SEGMENTS, LANES), F32)
    o = jnp.ones((SEGMENTS, LANES), F32)
    hf, af, hb, ab = lax.fori_loop(0, seg_len, pass1, (z, o, z, o), unroll=SCAN_UNROLL)

    if has_h0:
        cf = h0_ref[0:1, :]
        cbk = h0_ref[1:2, :]
    else:
        cf = jnp.zeros((1, LANES), F32)
        cbk = jnp.zeros((1, LANES), F32)
    sub = lax.broadcasted_iota(jnp.int32, (SEGMENTS, LANES), 0)
    carry_f = z
    for j in range(SEGMENTS):
        carry_f = jnp.where(sub == j, cf, carry_f)
        cf = hf[j:j + 1, :] + af[j:j + 1, :] * cf
    carry_b = z
    for j in reversed(range(SEGMENTS)):
        carry_b = jnp.where(sub == j, cbk, carry_b)
        cbk = hb[j:j + 1, :] + ab[j:j + 1, :] * cbk
    fin_ref[0:1, :] = cf
    fin_ref[1:2, :] = cbk

    def pass2(t, carry):
        hf, hb = carry
        sb = seg_len - 1 - t
        hf = a_ref[0, rows(t), :] * hf + b_ref[0, rows(t), :]
        h_ref[0, rows(t), :] = hf
        hb = a_ref[1, rows(sb), :] * hb + b_ref[1, rows(sb), :]
        h_ref[1, rows(sb), :] = hb
        return hf, hb

    lax.fori_loop(0, seg_len, pass2, (carry_f, carry_b), unroll=SCAN_UNROLL)

    for t in range(n // tile):
        r0 = t * tile
        seg = r0 // seg_len
        s0 = r0 - seg * seg_len
        src = pl.ds(s0 * SEGMENTS + seg, tile, stride=SEGMENTS)
        hsum = h_ref[0, src, :] + h_ref[1, src, :]
        y_ref[pl.ds(r0, tile), :] = hsum * jax.nn.gelu(ug_ref[pl.ds(r0, tile), :])


def _lru_call(ux, ug, lp, h0):
    b, n, _ = ux.shape
    nblk = D_LRU // LANES
    has_h0 = h0 is not None
    blk = pl.BlockSpec((None, n, LANES), lambda i, j: (i, 0, j))
    in_specs = [
        blk, blk,
        pl.BlockSpec((CONV_W, LANES), lambda i, j: (0, j)),
        pl.BlockSpec((1, LANES), lambda i, j: (0, j)),
        pl.BlockSpec((None, LANES, 4 * LANES), lambda i, j: (j, 0, 0)),
        pl.BlockSpec((None, 1, 4 * LANES), lambda i, j: (j, 0, 0)),
        pl.BlockSpec((2, LANES), lambda i, j: (0, j)),
    ]
    args = [ux, ug, lp["conv_w"], lp["conv_b"], lp["w_gates"], lp["b_gates"], lp["lam"]]
    if has_h0:
        in_specs.append(pl.BlockSpec((None, 2, LANES), lambda i, j: (i, 0, j)))
        args.append(h0)
    return pl.pallas_call(
        functools.partial(_lru_kernel, has_h0),
        grid=(b, nblk),
        in_specs=in_specs,
        out_specs=[blk, pl.BlockSpec((None, 2, LANES), lambda i, j: (i, 0, j))],
        out_shape=[jax.ShapeDtypeStruct((b, n, D_LRU), F32),
                   jax.ShapeDtypeStruct((b, 2, D_LRU), F32)],
        scratch_shapes=[
            pltpu.VMEM((n + 2 * PAD_ROWS, LANES), F32),
            pltpu.VMEM((2, n, LANES), F32),
            pltpu.VMEM((2, n, LANES), F32),
            pltpu.VMEM((2, n, LANES), F32),
        ],
        compiler_params=_cparams(("arbitrary", "arbitrary")),
        name="lru",
    )(*args)


def _lru_params(conv_w, conv_b, wa, ba, wi, bi, lam):
    nblk = D_LRU // LANES
    hpb = LANES // LRU_HEAD_DIM

    def blockdiag(w):
        w = w.reshape(nblk, hpb, LRU_HEAD_DIM, LRU_HEAD_DIM)
        eye = jnp.eye(hpb, dtype=w.dtype)
        return jnp.einsum("bhij,hg->bhigj", w, eye).reshape(nblk, LANES, LANES)

    w_gates = jnp.concatenate([blockdiag(wa[0]), blockdiag(wi[0]), blockdiag(wa[1]), blockdiag(wi[1])], axis=-1)
    b_gates = jnp.stack([ba[0], bi[0], ba[1], bi[1]], axis=0)
    b_gates = b_gates.reshape(4, nblk, LANES).transpose(1, 0, 2).reshape(nblk, 1, 4 * LANES)
    return {"conv_w": conv_w, "conv_b": conv_b.reshape(1, D_LRU), "w_gates": w_gates,
            "b_gates": b_gates, "lam": lam}


def _mixout_kernel(x_ref, yf_ref, yr_ref, w_ref, g1_ref, n2_ref, sh_ref, sc_ref, wr_ref,
                   xo_ref, h2_ref, lg_ref, wbf_ref):
    @pl.when((pl.program_id(0) == 0) & (pl.program_id(1) == 0))
    def _cast_w():
        wbf_ref[...] = w_ref[...].astype(BF16)

    mix = jnp.dot(yf_ref[...].astype(BF16), wbf_ref[:D_FOURIER, :], preferred_element_type=F32)
    mix = mix + jnp.dot(yr_ref[...].astype(BF16), wbf_ref[D_FOURIER:, :], preferred_element_type=F32)
    x = x_ref[...] + g1_ref[...] * mix
    xo_ref[...] = x
    h2 = _rms_mod(x, n2_ref[...], sh_ref[...], sc_ref[...])
    tm = h2.shape[0]
    for s in range(TOKEN_ROWS):
        h2_ref[pl.ds(s, tm, stride=TOKEN_ROWS), :] = h2[:, s * LANES:(s + 1) * LANES]
    lg_ref[...] = lax.dot_general(wr_ref[...], h2, (((1,), (1,)), ((), ())),
                                  preferred_element_type=F32, precision=lax.Precision.HIGHEST)


def _mixout_call(x, yf, yr, w_out, g1, n2g, sh2, sc2, w_router_t, layer=0):
    b, n, d = x.shape
    tm = min(ROW_TILE, n)
    row = pl.BlockSpec((None, tm, d), lambda i, j: (i, j, 0))
    half = pl.BlockSpec((None, tm, D_FOURIER), lambda i, j: (i, j, 0))
    vec_b = pl.BlockSpec((None, 1, d), lambda i, j: (i, 0, 0))
    return pl.pallas_call(
        _mixout_kernel,
        grid=(b, n // tm),
        in_specs=[row, half, half,
                  _layer_weight_spec(w_out, layer),
                  vec_b,
                  pl.BlockSpec((1, d), lambda i, j: (0, 0)),
                  vec_b, vec_b,
                  pl.BlockSpec((N_EXPERTS, d), lambda i, j: (0, 0))],
        out_specs=[row,
                   pl.BlockSpec((tm * TOKEN_ROWS, LANES), lambda i, j: (i * (n // tm) + j, 0)),
                   pl.BlockSpec((None, N_EXPERTS, tm), lambda i, j: (i, 0, j))],
        out_shape=[jax.ShapeDtypeStruct((b, n, d), F32),
                   jax.ShapeDtypeStruct((b * n * TOKEN_ROWS, LANES), F32),
                   jax.ShapeDtypeStruct((b, N_EXPERTS, n), F32)],
        scratch_shapes=[pltpu.VMEM((d, d), BF16)],
        compiler_params=_cparams(("arbitrary", "arbitrary")),
        name="mixout",
    )(x, yf, yr, w_out, g1, n2g.reshape(1, d), sh2, sc2, w_router_t)


def _lane_prefix_sum(x):
    n = x.shape[-1]
    lane = lax.broadcasted_iota(jnp.int32, x.shape, x.ndim - 1)
    s = 1
    while s < n:
        x = x + jnp.where(lane >= s, pltpu.roll(x, s, x.ndim - 1), 0)
        s *= 2
    return x


BISECT_STEPS = 32
BRACKET = 2.0 ** -6
TINY = 1e-37


def _route_kernel(cap, lg_ref, key_ref, csum_ref, keyt_ref, afft_ref):
    n = lg_ref.shape[1]
    lg = lg_ref[...]
    m = jnp.max(lg, axis=0, keepdims=True)
    ex = jnp.exp(lg - m)
    aff = ex / jnp.sum(ex, axis=0, keepdims=True)
    bits = pltpu.bitcast(aff, jnp.int32)

    def bis(_, carry):
        lo, hi = carry
        mid = lo + ((hi - lo + 1) >> 1)
        cnt = jnp.sum((bits >= mid).astype(jnp.int32), axis=1, keepdims=True)
        ok = cnt >= cap
        return jnp.where(ok, mid, lo), jnp.where(ok, hi, mid - 1)

    lo0 = jnp.zeros((N_EXPERTS, 1), jnp.int32)
    hi0 = jnp.full((N_EXPERTS, 1), 0x7F800000, jnp.int32)
    thr, _ = lax.fori_loop(0, BISECT_STEPS, bis, (lo0, hi0))

    thr_f = pltpu.bitcast(thr, F32)
    flo0 = thr_f * (1.0 - BRACKET)
    fhi0 = jnp.maximum(thr_f * (1.0 + BRACKET), jnp.float32(TINY))

    def fbis(_, carry):
        lo, hi = carry
        mid = lo + 0.5 * (hi - lo)
        cnt = jnp.sum((aff >= mid).astype(jnp.int32), axis=1, keepdims=True)
        ok = cnt >= cap
        return jnp.where(ok, mid, lo), jnp.where(ok, hi, mid)

    flo, fhi = lax.fori_loop(0, BISECT_STEPS, fbis, (flo0, fhi0))
    gt = aff >= fhi
    eq = (aff >= flo) & (aff < fhi)
    need = cap - jnp.sum(gt.astype(jnp.int32), axis=1, keepdims=True)
    eq_rank = _lane_prefix_sum(eq.astype(jnp.int32)) - eq.astype(jnp.int32)
    sel = gt | (eq & (eq_rank < need))
    csum = _lane_prefix_sum(sel.astype(jnp.int32))
    key = jnp.where(sel, csum - 1, -1)
    key_ref[...] = key
    csum_ref[...] = csum

    pad = jnp.zeros((LANES - N_EXPERTS, n), F32)
    keyt_ref[...] = jnp.concatenate([key.astype(F32), pad - 1.0], axis=0).T
    afft_ref[...] = jnp.concatenate([aff, pad], axis=0).T


def _route_call(logits_t, cap):
    b, _, n = logits_t.shape
    en = pl.BlockSpec((None, N_EXPERTS, n), lambda i: (i, 0, 0))
    tm = pl.BlockSpec((None, n, LANES), lambda i: (i, 0, 0))
    return pl.pallas_call(
        functools.partial(_route_kernel, cap),
        grid=(b,),
        in_specs=[en],
        out_specs=[en, en, tm, tm],
        out_shape=[jax.ShapeDtypeStruct((b, N_EXPERTS, n), jnp.int32),
                   jax.ShapeDtypeStruct((b, N_EXPERTS, n), jnp.int32),
                   jax.ShapeDtypeStruct((b, n, LANES), F32),
                   jax.ShapeDtypeStruct((b, n, LANES), F32)],
        compiler_params=_cparams(("arbitrary",)),
        name="route",
    )(logits_t)


COMPACT_TILE = 64


def _compact_kernel(cap, klo_ref, khi_ref, key_ref, idx_ref):
    b = pl.program_id(0)
    nblk = key_ref.shape[1]
    rt = min(COMPACT_TILE, cap)
    ntile = cap // rt
    width = min(LANES, cap)
    per_store = width // rt
    lane = lax.broadcasted_iota(jnp.int32, (1, LANES), 1)
    sub = lax.broadcasted_iota(jnp.int32, (rt, 1), 0)
    base = b * (nblk * LANES)
    zero = jnp.zeros((rt, LANES), F32)
    for e in range(N_EXPERTS):
        row = jnp.zeros((1, LANES), F32)
        for t in range(ntile):
            s = (b * N_EXPERTS + e) * ntile + t
            rr = t * rt + sub

            def block(k, ids, e=e, rr=rr):
                p = key_ref[e, pl.ds(k, 1), :] == rr
                return ids + jnp.where(p, (base + k * LANES + lane).astype(F32), 0.0)

            ids = lax.fori_loop(klo_ref[s], khi_ref[s], block, zero)
            col = jnp.sum(ids, axis=1, keepdims=True)
            diag = (sub + (t % per_store) * rt) == lane
            row = row + jnp.sum(jnp.where(diag, col, 0.0), axis=0, keepdims=True)
            if (t + 1) % per_store == 0:
                g = t // per_store
                idx_ref[e:e + 1, g * width:(g + 1) * width] = row[:, :width].astype(jnp.int32)
                row = jnp.zeros((1, LANES), F32)


def _compact_call(key, csum, cap):
    b, _, n = key.shape
    nblk = n // LANES
    rt = min(COMPACT_TILE, cap)
    ntile = cap // rt
    cb_incl = csum[:, :, LANES - 1::LANES]
    cb_excl = jnp.concatenate([jnp.zeros_like(cb_incl[..., :1]), cb_incl[..., :-1]], axis=-1)
    t0 = (jnp.arange(ntile, dtype=jnp.int32) * rt)[:, None]
    klo = jnp.sum((cb_incl[:, :, None, :] <= t0).astype(jnp.int32), axis=-1)
    khi = jnp.sum((cb_excl[:, :, None, :] < t0 + rt).astype(jnp.int32), axis=-1)
    return pl.pallas_call(
        functools.partial(_compact_kernel, cap),
        grid_spec=pltpu.PrefetchScalarGridSpec(
            num_scalar_prefetch=2, grid=(b,),
            in_specs=[pl.BlockSpec((None, N_EXPERTS, nblk, LANES), lambda i, *_: (i, 0, 0, 0))],
            out_specs=pl.BlockSpec((None, N_EXPERTS, cap), lambda i, *_: (i, 0, 0))),
        out_shape=jax.ShapeDtypeStruct((b, N_EXPERTS, cap), jnp.int32),
        compiler_params=_cparams(("arbitrary",)),
        name="compact",
    )(klo.reshape(-1), khi.reshape(-1), key.reshape(b, N_EXPERTS, nblk, LANES))


FF_TILE = 256


FF_STEPS = D_FF // FF_TILE
ROW_ALIGN = 32


def _round_up(v, m):
    return -(-v // m) * m


def _ffn_layout(row_counts):
    layout, off = [], 0
    for m in row_counts:
        padded = _round_up(m, FF_STEPS * SUBLANES)
        layout.append((m, padded, off))
        off = _round_up(off + padded, ROW_ALIGN)
    out_rows = _round_up(layout[-1][2] + layout[-1][0], ROW_ALIGN)
    scratch_rows = _round_up(layout[-1][2] + layout[-1][1], SUBLANES)
    return tuple(layout), out_rows, max(scratch_rows, out_rows)


def _ffn_kernel(layout, *refs):
    ns = len(layout)
    idx_refs = refs[:ns]
    tabs = refs[ns:2 * ns]
    wg_ref, wu_ref, wd_ref, o_ref, xg32_ref, xg_ref, acc_ref, sem = refs[2 * ns:]
    e = pl.program_id(0)
    f = pl.program_id(1)
    n_e = pl.num_programs(0)
    slot = e % 2

    def issue_chunk(expert, dst_slot, step):
        for s, (_, padded, off) in enumerate(layout):
            chunk = padded // FF_STEPS
            src0 = expert * padded + step * chunk
            dst0 = off + step * chunk

            def body(g, carry, s=s, src0=src0, dst0=dst0):
                src = src0 + g * SUBLANES
                dst = pl.multiple_of(dst0 + g * SUBLANES, SUBLANES)
                for j in range(SUBLANES):
                    tok = idx_refs[s][src + j]
                    pltpu.make_async_copy(
                        tabs[s].at[pl.ds(pl.multiple_of(tok * TOKEN_ROWS, TOKEN_ROWS), TOKEN_ROWS), :],
                        xg32_ref.at[dst_slot, pl.ds(pl.multiple_of((dst + j) * TOKEN_ROWS, TOKEN_ROWS), TOKEN_ROWS), :],
                        sem.at[dst_slot]).start()
                return carry
            lax.fori_loop(0, chunk // SUBLANES, body, 0)

    @pl.when((e == 0) & (f == 0))
    def _first_expert():
        for step in range(FF_STEPS):
            issue_chunk(0, 0, step)

    @pl.when(f == 0)
    def _start_expert():
        for s, (m, padded, off) in enumerate(layout):
            pltpu.make_async_copy(tabs[s].at[pl.ds(0, padded * TOKEN_ROWS), :],
                                  xg32_ref.at[slot, pl.ds(off * TOKEN_ROWS, padded * TOKEN_ROWS), :],
                                  sem.at[slot]).wait()
        pack = 2 * SUBLANES
        for m, _, off in layout:
            def unpack(g, carry, off=off):
                t0 = pl.multiple_of(off + g * pack, pack)
                for c in range(TOKEN_ROWS):
                    v = xg32_ref[slot, pl.ds(t0 * TOKEN_ROWS + c, pack, stride=TOKEN_ROWS), :]
                    xg_ref[pl.ds(t0, pack), c * LANES:(c + 1) * LANES] = v.astype(BF16)
                return carry
            lax.fori_loop(0, m // pack, unpack, 0)
            acc_ref[pl.ds(off, m), :] = jnp.zeros((m, acc_ref.shape[1]), F32)

    nxt = jnp.minimum(e + 1, n_e - 1)
    other = 1 - slot
    copies = []
    for s, (_, padded, off) in enumerate(layout):
        chunk = padded // FF_STEPS
        for r in range(chunk):
            copies.append((s, nxt * padded + f * chunk + r, off + f * chunk, r))
    tiles = [(off + i * min(ROW_TILE, m), min(ROW_TILE, m))
             for m, _, off in layout for i in range(m // min(ROW_TILE, m))]
    share = -(-len(copies) // len(tiles))

    wg = wg_ref[...].astype(BF16)
    wu = wu_ref[...].astype(BF16)
    wd = wd_ref[...].astype(BF16)
    for ti, (r0, mt) in enumerate(tiles):
        for s, src, dst0, r in copies[ti * share:(ti + 1) * share]:
            tok = idx_refs[s][src]
            dst = pl.multiple_of(dst0 * TOKEN_ROWS, SUBLANES * TOKEN_ROWS) + r * TOKEN_ROWS
            pltpu.make_async_copy(
                tabs[s].at[pl.ds(pl.multiple_of(tok * TOKEN_ROWS, TOKEN_ROWS), TOKEN_ROWS), :],
                xg32_ref.at[other, pl.ds(pl.multiple_of(dst, TOKEN_ROWS), TOKEN_ROWS), :],
                sem.at[other]).start()
        rows = pl.ds(r0, mt)
        xg = xg_ref[rows, :]
        g = jnp.dot(xg, wg, preferred_element_type=F32)
        u = jnp.dot(xg, wu, preferred_element_type=F32)
        hid = (g * jax.nn.sigmoid(g) * u).astype(BF16)
        acc_ref[rows, :] += jnp.dot(hid, wd, preferred_element_type=F32)

    @pl.when((e == n_e - 1) & (f == FF_STEPS - 1))
    def _drain():
        for s, (_, padded, off) in enumerate(layout):
            pltpu.make_async_copy(tabs[s].at[pl.ds(0, padded * TOKEN_ROWS), :],
                                  xg32_ref.at[other, pl.ds(off * TOKEN_ROWS, padded * TOKEN_ROWS), :],
                                  sem.at[other]).wait()

    @pl.when(f == FF_STEPS - 1)
    def _finish():
        end = 0
        for m, _, off in layout:
            if off > end:
                o_ref[pl.ds(end, off - end), :] = jnp.zeros((off - end, o_ref.shape[1]), o_ref.dtype)
            o_ref[pl.ds(off, m), :] = acc_ref[pl.ds(off, m), :].astype(o_ref.dtype)
            end = off + m
        if o_ref.shape[0] > end:
            o_ref[pl.ds(end, o_ref.shape[0] - end), :] = jnp.zeros((o_ref.shape[0] - end, o_ref.shape[1]), o_ref.dtype)


def _ffn_call(layer, row_counts, idxs, tables, w_gate, w_up, w_down):
    ns = len(tables)
    n_e = w_gate.shape[1]
    d = w_gate.shape[2]
    layout, out_rows, scratch_rows = _ffn_layout(row_counts)
    grid_spec = pltpu.PrefetchScalarGridSpec(
        num_scalar_prefetch=ns,
        grid=(n_e, FF_STEPS),
        in_specs=[pl.BlockSpec(memory_space=pl.ANY)] * ns
        + [pl.BlockSpec((None, None, d, FF_TILE), lambda e, f, *_: (layer, e, 0, f)),
           pl.BlockSpec((None, None, d, FF_TILE), lambda e, f, *_: (layer, e, 0, f)),
           pl.BlockSpec((None, None, FF_TILE, d), lambda e, f, *_: (layer, e, f, 0))],
        out_specs=pl.BlockSpec((None, out_rows, d), lambda e, f, *_: (e, 0, 0)),
        scratch_shapes=[pltpu.VMEM((2, scratch_rows * TOKEN_ROWS, LANES), F32), pltpu.VMEM((scratch_rows, d), BF16),
                        pltpu.VMEM((scratch_rows, d), F32), pltpu.SemaphoreType.DMA((2,))],
    )
    return pl.pallas_call(
        functools.partial(_ffn_kernel, layout),
        grid_spec=grid_spec,
        out_shape=jax.ShapeDtypeStruct((n_e, out_rows, d), BF16),
        compiler_params=_cparams(("arbitrary", "arbitrary")),
        name="ffn",
    )(*idxs, *tables, w_gate, w_up, w_down)


COMBINE_TOKENS = ROW_TILE
COMBINE_WINDOW = 256
ROW_PACK = 16


def _combine_kernel(final, win, start_ref, spill_ref, *refs):
    if final:
        x_ref, kt_ref, at_ref, y_ref, g2_ref, fg_ref, o_ref, acc_ref = refs
    else:
        x_ref, kt_ref, at_ref, y_ref, g2_ref, o_ref, acc_ref = refs
    cap = y_ref.shape[1]
    windowed = win < cap
    step = (pl.program_id(0) * pl.num_programs(1) + pl.program_id(1)) * N_EXPERTS
    kt = kt_ref[...]
    at = at_ref[...]
    lane = lax.broadcasted_iota(jnp.int32, (1, win), 1).astype(F32)

    acc = jnp.zeros(x_ref.shape, F32)
    for e in range(N_EXPERTS):
        if windowed:
            s0 = pl.multiple_of(start_ref[step + e], ROW_PACK)
            q = (kt[:, e:e + 1] == lane + s0.astype(F32)).astype(BF16)
            part = jnp.dot(q, y_ref[e, pl.ds(s0, win), :], preferred_element_type=F32)
        else:
            q = (kt[:, e:e + 1] == lane).astype(BF16)
            part = jnp.dot(q, y_ref[e], preferred_element_type=F32)
        acc = acc + at[:, e:e + 1] * part

    if windowed:
        acc_ref[...] = acc
        for e in range(N_EXPERTS):
            @pl.when(spill_ref[step + e] != 0)
            def _rest(e=e):
                done = (start_ref[step + e] + win).astype(F32)
                ke = kt_ref[:, e:e + 1]
                ke = jnp.where(ke >= done, ke, -1.0)
                q = (ke == lane + float(cap - win)).astype(BF16)
                acc_ref[...] += at_ref[:, e:e + 1] * jnp.dot(q, y_ref[e, cap - win:, :],
                                                             preferred_element_type=F32)
        acc = acc_ref[...]

    x = x_ref[...] + g2_ref[...] * acc
    if final:
        x = x * lax.rsqrt(jnp.mean(x * x, axis=-1, keepdims=True) + EPS) * fg_ref[...]
    o_ref[...] = x


def _combine_call(x, keyt, afft, csum, yexp, first_row, cap, g2, final_g):
    b, n, d = x.shape
    tc = min(COMBINE_TOKENS, n)
    win = min(COMBINE_WINDOW, cap)
    assert cap in (win, 2 * win)
    final = final_g is not None
    blk0 = first_row // cap
    upto = csum[:, :, tc - 1::tc]
    before = jnp.concatenate([jnp.zeros_like(upto[..., :1]), upto[..., :-1]], axis=-1)
    start = jnp.clip((before // ROW_PACK) * ROW_PACK, 0, cap - win)
    spill = (upto > start + win).astype(jnp.int32)
    start = start.transpose(0, 2, 1).reshape(-1)
    spill = spill.transpose(0, 2, 1).reshape(-1)
    row = pl.BlockSpec((None, tc, d), lambda i, j, *_: (i, j, 0))
    tok = pl.BlockSpec((None, tc, LANES), lambda i, j, *_: (i, j, 0))
    in_specs = [row, tok, tok,
                pl.BlockSpec((N_EXPERTS, cap, d), lambda i, j, *_: (0, blk0 + i, 0)),
                pl.BlockSpec((None, 1, d), lambda i, j, *_: (i, 0, 0))]
    args = [x, keyt, afft, yexp, g2]
    if final:
        in_specs.append(pl.BlockSpec((1, d), lambda i, j, *_: (0, 0)))
        args.append(final_g.reshape(1, d))
    return pl.pallas_call(
        functools.partial(_combine_kernel, final, win),
        grid_spec=pltpu.PrefetchScalarGridSpec(
            num_scalar_prefetch=2, grid=(b, n // tc), in_specs=in_specs, out_specs=row,
            scratch_shapes=[pltpu.VMEM((tc, d), F32)]),
        out_shape=jax.ShapeDtypeStruct((b, n, d), F32),
        compiler_params=_cparams(("arbitrary", "arbitrary")),
        name="combine",
    )(start, spill, *args)


def _grid_pos_embed(rows, d):
    rr, cc = np.meshgrid(np.arange(rows, dtype=np.float64), np.arange(GRID_W, dtype=np.float64), indexing="ij")
    quarter = d // 4
    omega = 1.0 / (POS_BASE ** (np.arange(quarter, dtype=np.float64) / quarter))

    def enc(p):
        ang = p.reshape(-1)[:, None] * omega
        return np.concatenate([np.sin(ang), np.cos(ang)], axis=-1)

    return np.concatenate([enc(rr), enc(cc)], axis=-1).astype(np.float32)


def _moe(layer, streams, w_gate, w_up, w_down):
    d = streams[0]["x"].shape[-1]
    caps = [EC_CAPACITY * s["x"].shape[1] // N_EXPERTS for s in streams]
    row_counts = [s["x"].shape[0] * cap for s, cap in zip(streams, caps)]
    layout, _, _ = _ffn_layout(row_counts)
    idxs, routes = [], []
    for s, cap, (m, padded, off) in zip(streams, caps, layout):
        key, csum, keyt, afft = _route_call(s["logits_t"], cap)
        idx = _compact_call(key, csum, cap)
        idx = idx.transpose(1, 0, 2).reshape(N_EXPERTS, m)
        idxs.append(jnp.pad(idx, ((0, 0), (0, padded - m))).reshape(N_EXPERTS * padded))
        routes.append((keyt, afft, csum))
    tables = [s["h2"] for s in streams]
    yexp = _ffn_call(layer, row_counts, idxs, tables, w_gate, w_up, w_down)
    return [_combine_call(s["x"], keyt, afft, csum, yexp, off, cap, s["g2"], s["final_g"])
            for s, (keyt, afft, csum), cap, (_, _, off) in zip(streams, routes, caps, layout)]


def kernel(x, c, ctx, c_ctx, w_mod, b_mod, norm1_g, norm2_g, w_in, conv_w, conv_b, lru_wa, lru_ba,
           lru_wi, lru_bi, lru_lambda, w_out, w_router, w_gate, w_up, w_down, final_g):
    bsz, n, d = x.shape
    depth = w_mod.shape[0]
    pos = jnp.asarray(_grid_pos_embed(n // GRID_W, d))

    crows = jnp.zeros((SUBLANES, d), F32).at[:bsz].set(c).at[bsz].set(c_ctx)
    mod = _mod_call(crows, w_mod, b_mod)

    for l in range(depth):
        last = l == depth - 1
        mx = mod[l, :bsz].reshape(bsz, 1, N_MOD, d)
        sh1, sc1, g1, sh2, sc2, g2 = [mx[:, :, i, :] for i in range(N_MOD)]
        mc = jnp.broadcast_to(mod[l, bsz].reshape(1, 1, N_MOD, d), (bsz, 1, N_MOD, d))
        csh1, csc1, cg1, csh2, csc2, cg2 = [mc[:, :, i, :] for i in range(N_MOD)]
        w_router_t = w_router[l].T
        lp = _lru_params(conv_w[l], conv_b[l], lru_wa[l], lru_ba[l], lru_wi[l], lru_bi[l], lru_lambda[l])

        cuf, cux, cug = _proj_call(ctx, None, norm1_g[l], csh1, csc1, w_in, l)
        cyr, ctx_states = _lru_call(cux, cug, lp, None)
        streams = []
        if not last:
            cyf = _fourier_call(cuf)
            ctx, ch2, clg = _mixout_call(ctx, cyf, cyr, w_out, cg1, norm2_g[l], csh2, csc2, w_router_t, l)
            streams.append(dict(x=ctx, h2=ch2, logits_t=clg, g2=cg2, final_g=None))

        if l == 0:
            uf, ux, ug, x = _proj_call(x, pos, norm1_g[l], sh1, sc1, w_in, l)
        else:
            uf, ux, ug = _proj_call(x, None, norm1_g[l], sh1, sc1, w_in, l)
        yf = _fourier_call(uf)
        yr, _ = _lru_call(ux, ug, lp, ctx_states)
        x, h2, lg = _mixout_call(x, yf, yr, w_out, g1, norm2_g[l], sh2, sc2, w_router_t, l)
        streams.insert(0, dict(x=x, h2=h2, logits_t=lg, g2=g2, final_g=final_g if last else None))
        outs = _moe(l, streams, w_gate, w_up, w_down)
        x = outs[0]
        if not last:
            ctx = outs[1]
    return x
```

```python
import functools
import math

import numpy as np
import jax
import jax.numpy as jnp
from jax import lax
from jax.experimental import pallas as pl
from jax.experimental.pallas import tpu as pltpu

D_MODEL = 1024
GRID_W = 64
POS_BASE = 10000.0
D_FOURIER = 512
FOURIER_GROUPS = 4
D_GROUP = D_FOURIER // FOURIER_GROUPS
D_LRU = 512
LRU_HEADS = 8
LRU_HEAD_DIM = D_LRU // LRU_HEADS
CONV_W = 4
LRU_C = 8.0
D_IN = D_FOURIER + 2 * D_LRU
N_EXPERTS = 16
EC_CAPACITY = 2
D_FF = 2816
N_MOD = 6
EPS = 1e-6

LANES = 128
SUBLANES = 8
DFT_BLOCK = 256
TOKEN_ROWS = D_MODEL // LANES
ROW_TILE = 512
VMEM_LIMIT = 56 * 1024 * 1024

F32 = jnp.float32
BF16 = jnp.bfloat16


def _cparams(sem, **kw):
    return pltpu.CompilerParams(dimension_semantics=sem, vmem_limit_bytes=VMEM_LIMIT, **kw)


def _mod_kernel(c_ref, w_ref, b_ref, o_ref):
    c = c_ref[...]
    cond = c * jax.nn.sigmoid(c)
    o_ref[...] = jnp.dot(cond, w_ref[...], preferred_element_type=F32,
                         precision=lax.Precision.HIGHEST) + b_ref[...]


def _mod_call(crows, w_mod, b_mod):
    depth, d, nm = w_mod.shape
    tn = 1536
    return pl.pallas_call(
        _mod_kernel,
        grid=(depth, nm // tn),
        in_specs=[
            pl.BlockSpec((SUBLANES, d), lambda l, j: (0, 0)),
            pl.BlockSpec((None, d, tn), lambda l, j: (l, 0, j)),
            pl.BlockSpec((None, 1, tn), lambda l, j: (l, 0, j)),
        ],
        out_specs=pl.BlockSpec((None, SUBLANES, tn), lambda l, j: (l, 0, j)),
        out_shape=jax.ShapeDtypeStruct((depth, SUBLANES, nm), F32),
        compiler_params=_cparams(("arbitrary", "arbitrary")),
        name="mod",
    )(crows, w_mod, b_mod.reshape(depth, 1, nm))


def _rms_mod(x, g, shift, scale):
    y = x * lax.rsqrt(jnp.mean(x * x, axis=-1, keepdims=True) + EPS) * g
    return y * (1.0 + scale) + shift


def _proj_kernel(add_pos, *refs):
    if add_pos:
        x_ref, pos_ref, g_ref, sh_ref, sc_ref, w_ref, uf_ref, ux_ref, ug_ref, xp_ref, wbf_ref = refs
        x = x_ref[...] + pos_ref[...]
        xp_ref[...] = x
    else:
        x_ref, g_ref, sh_ref, sc_ref, w_ref, uf_ref, ux_ref, ug_ref, wbf_ref = refs
        x = x_ref[...]

    @pl.when((pl.program_id(0) == 0) & (pl.program_id(1) == 0))
    def _cast_w():
        wbf_ref[...] = w_ref[...].astype(BF16)

    h = _rms_mod(x, g_ref[...], sh_ref[...], sc_ref[...])
    u = jnp.dot(h.astype(BF16), wbf_ref[...], preferred_element_type=F32)
    uf_ref[...] = u[:, :D_FOURIER]
    ux_ref[...] = u[:, D_FOURIER:D_FOURIER + D_LRU]
    ug_ref[...] = u[:, D_FOURIER + D_LRU:]


def _layer_weight_spec(w, layer):
    rows, cols = w.shape[-2:]
    if w.ndim == 2:
        return pl.BlockSpec((rows, cols), lambda i, j: (0, 0))
    return pl.BlockSpec((None, rows, cols), lambda i, j: (layer, 0, 0))


def _proj_call(x, pos, g, shift, scale, w_in, layer=0):
    b, n, d = x.shape
    tm = min(ROW_TILE, n)
    add_pos = pos is not None
    row = pl.BlockSpec((None, tm, d), lambda i, j: (i, j, 0))
    vec_b = pl.BlockSpec((None, 1, d), lambda i, j: (i, 0, 0))
    in_specs = [row]
    args = [x]
    if add_pos:
        in_specs.append(pl.BlockSpec((tm, d), lambda i, j: (j, 0)))
        args.append(pos)
    in_specs += [pl.BlockSpec((1, d), lambda i, j: (0, 0)), vec_b, vec_b,
                 _layer_weight_spec(w_in, layer)]
    args += [g.reshape(1, d), shift, scale, w_in]
    part = pl.BlockSpec((None, tm, D_FOURIER), lambda i, j: (i, j, 0))
    out_specs = [part, part, part]
    out_shape = [jax.ShapeDtypeStruct((b, n, D_FOURIER), F32)] * 3
    if add_pos:
        out_specs.append(row)
        out_shape.append(jax.ShapeDtypeStruct((b, n, d), F32))
    return pl.pallas_call(
        functools.partial(_proj_kernel, add_pos),
        grid=(b, n // tm),
        in_specs=in_specs,
        out_specs=out_specs,
        out_shape=out_shape,
        scratch_shapes=[pltpu.VMEM((d, D_IN), BF16)],
        compiler_params=_cparams(("arbitrary", "arbitrary")),
        name="proj",
    )(*args)


def _cmul_const(re, im, c, s):
    tol = 1e-12
    if abs(s) < tol:
        if abs(c - 1.0) < tol:
            return re, im
        if abs(c + 1.0) < tol:
            return -re, -im
        return re * c, im * c
    if abs(c) < tol:
        if abs(s - 1.0) < tol:
            return -im, re
        if abs(s + 1.0) < tol:
            return im, -re
        return -im * s, re * s
    return re * c - im * s, re * s + im * c


def _fft_list(xs):
    n = len(xs)
    if n == 1:
        return xs
    ev = _fft_list(xs[0::2])
    od = _fft_list(xs[1::2])
    out = [None] * n
    for k in range(n // 2):
        ang = -2.0 * math.pi * k / n
        tr, ti = _cmul_const(od[k][0], od[k][1], math.cos(ang), math.sin(ang))
        out[k] = (ev[k][0] + tr, ev[k][1] + ti)
        out[k + n // 2] = (ev[k][0] - tr, ev[k][1] - ti)
    return out


def _fourier_kernel(n1_count, u_ref, cs_ref, twc_ref, tws_ref, cc_ref, o_ref, tr_ref, ti_ref, pq_ref):
    n = u_ref.shape[0]
    cs = cs_ref[...].astype(BF16)
    for n1 in range(n1_count):
        if n1_count == 1:
            z = u_ref[...]
        else:
            z = u_ref[pl.ds(n1, DFT_BLOCK, stride=n1_count), :]
        g = jnp.dot(cs, z.astype(BF16), preferred_element_type=F32)
        gr = g[:DFT_BLOCK]
        gs = g[DFT_BLOCK:]
        if n1 == 0:
            tr_ref[n1] = gr
            ti_ref[n1] = -gs
        else:
            c = twc_ref[n1]
            s = tws_ref[n1]
            tr_ref[n1] = gr * c - gs * s
            ti_ref[n1] = -(gr * s + gs * c)

    if n1_count == 1:
        pq_ref[:, :D_GROUP] = tr_ref[0]
        pq_ref[:, D_GROUP:] = -ti_ref[0]
    else:
        def chunk(j, carry):
            r0 = pl.multiple_of(j * SUBLANES, SUBLANES)
            xs = [(tr_ref[i, pl.ds(r0, SUBLANES), :], ti_ref[i, pl.ds(r0, SUBLANES), :])
                  for i in range(n1_count)]
            ys = _fft_list(xs)
            for k1 in range(n1_count):
                rows = pl.ds(pl.multiple_of(k1 * DFT_BLOCK + r0, SUBLANES), SUBLANES)
                pq_ref[rows, :D_GROUP] = ys[k1][0]
                pq_ref[rows, D_GROUP:] = -ys[k1][1]
            return carry
        lax.fori_loop(0, DFT_BLOCK // SUBLANES, chunk, 0)

    cc = cc_ref[...].astype(BF16)
    tile = min(ROW_TILE, n)
    for i in range(n // tile):
        rows = pl.ds(i * tile, tile)
        o_ref[rows, :] = jnp.dot(pq_ref[rows, :].astype(BF16), cc, preferred_element_type=F32)


@functools.lru_cache(maxsize=None)
def _fourier_consts(n):
    n1c = n // DFT_BLOCK
    k = np.arange(DFT_BLOCK, dtype=np.float64)
    ang = 2.0 * np.pi * np.outer(k, k) / DFT_BLOCK
    cs = np.concatenate([np.cos(ang), np.sin(ang)], axis=0)
    tw = 2.0 * np.pi * np.outer(np.arange(n1c, dtype=np.float64), k) / n
    twc = np.repeat(np.cos(tw)[:, :, None], LANES, axis=2)
    tws = np.repeat(np.sin(tw)[:, :, None], LANES, axis=2)
    c = np.arange(D_GROUP, dtype=np.float64)
    angc = 2.0 * np.pi * np.outer(c, c) / D_GROUP
    scale = 1.0 / math.sqrt(n * D_GROUP)
    cc = np.concatenate([np.cos(angc), -np.sin(angc)], axis=0) * scale
    return (cs.astype(np.float32), twc.astype(np.float32), tws.astype(np.float32), cc.astype(np.float32))


def _fourier_call(u):
    b, n, _ = u.shape
    n1c = n // DFT_BLOCK
    cs, twc, tws, cc = _fourier_consts(n)
    blk = pl.BlockSpec((None, n, D_GROUP), lambda i, j: (i, 0, j))
    return pl.pallas_call(
        functools.partial(_fourier_kernel, n1c),
        grid=(b, FOURIER_GROUPS),
        in_specs=[
            blk,
            pl.BlockSpec((2 * DFT_BLOCK, DFT_BLOCK), lambda i, j: (0, 0)),
            pl.BlockSpec((n1c, DFT_BLOCK, LANES), lambda i, j: (0, 0, 0)),
            pl.BlockSpec((n1c, DFT_BLOCK, LANES), lambda i, j: (0, 0, 0)),
            pl.BlockSpec((2 * D_GROUP, D_GROUP), lambda i, j: (0, 0)),
        ],
        out_specs=blk,
        out_shape=jax.ShapeDtypeStruct(u.shape, F32),
        scratch_shapes=[
            pltpu.VMEM((n1c, DFT_BLOCK, D_GROUP), F32),
            pltpu.VMEM((n1c, DFT_BLOCK, D_GROUP), F32),
            pltpu.VMEM((n, 2 * D_GROUP), F32),
        ],
        compiler_params=_cparams(("arbitrary", "arbitrary")),
        name="fourier",
    )(u, jnp.asarray(cs), jnp.asarray(twc), jnp.asarray(tws), jnp.asarray(cc))


SEGMENTS = SUBLANES
PAD_ROWS = 8
SCAN_UNROLL = 16


def _expm1(x):
    u = jnp.exp(x)
    near = (u - 1.0) * x / jnp.log(jnp.where(u == 1.0, 2.0, u))
    return jnp.where(x < -0.5, u - 1.0, jnp.where(u == 1.0, x, near))


def _lru_kernel(has_h0, *refs):
    if has_h0:
        (ux_ref, ug_ref, cw_ref, cb_ref, wg_ref, bg_ref, lam_ref, h0_ref,
         y_ref, fin_ref, upad_ref, a_ref, b_ref, h_ref) = refs
    else:
        (ux_ref, ug_ref, cw_ref, cb_ref, wg_ref, bg_ref, lam_ref,
         y_ref, fin_ref, upad_ref, a_ref, b_ref, h_ref) = refs
    n = ux_ref.shape[0]
    seg_len = n // SEGMENTS
    tile = min(256, seg_len)

    zeros_pad = jnp.zeros((PAD_ROWS, LANES), F32)
    upad_ref[pl.ds(0, PAD_ROWS), :] = zeros_pad
    upad_ref[pl.ds(PAD_ROWS + n, PAD_ROWS), :] = zeros_pad
    upad_ref[pl.ds(PAD_ROWS, n), :] = ux_ref[...]

    lam = lam_ref[...]
    sp = jnp.maximum(-lam, 0.0) + jnp.log1p(jnp.exp(-jnp.abs(lam)))
    cw = cw_ref[...]
    cb = cb_ref[...]
    wg = wg_ref[...].astype(BF16)
    bg = bg_ref[...]
    left = (CONV_W - 1) // 2

    for t in range(n // tile):
        r0 = t * tile
        xc = cb
        for k in range(CONV_W):
            xc = xc + upad_ref[pl.ds(PAD_ROWS + r0 + k - left, tile), :] * cw[k:k + 1, :]
        gates = jnp.dot(xc.astype(BF16), wg, preferred_element_type=F32) + bg
        seg = r0 // seg_len
        s0 = r0 - seg * seg_len
        for d in range(2):
            r = jax.nn.sigmoid(gates[:, (2 * d) * LANES:(2 * d + 1) * LANES])
            gi = jax.nn.sigmoid(gates[:, (2 * d + 1) * LANES:(2 * d + 2) * LANES])
            log_a = (-LRU_C) * r * sp[d:d + 1, :]
            a = jnp.exp(log_a)
            drive = jnp.sqrt(-_expm1(2.0 * log_a)) * (gi * xc)
            dst = pl.ds(s0 * SEGMENTS + seg, tile, stride=SEGMENTS)
            a_ref[d, dst, :] = a
            b_ref[d, dst, :] = drive

    def rows(s):
        return pl.ds(pl.multiple_of(s * SEGMENTS, SEGMENTS), SEGMENTS)

    def pass1(t, carry):
        hf, af, hb, ab = carry
        sb = seg_len - 1 - t
        a0 = a_ref[0, rows(t), :]
        a1 = a_ref[1, rows(sb), :]
        hf = a0 * hf + b_ref[0, rows(t), :]
        hb = a1 * hb + b_ref[1, rows(sb), :]
        return hf, af * a0, hb, ab * a1

    z = jnp.zeros((SEGMENTS, LANES), F32)
    o = jnp.ones((SEGMENTS, LANES), F32)
    hf, af, hb, ab = lax.fori_loop(0, seg_len, pass1, (z, o, z, o), unroll=SCAN_UNROLL)

    if has_h0:
        cf = h0_ref[0:1, :]
        cbk = h0_ref[1:2, :]
    else:
        cf = jnp.zeros((1, LANES), F32)
        cbk = jnp.zeros((1, LANES), F32)
    sub = lax.broadcasted_iota(jnp.int32, (SEGMENTS, LANES), 0)
    carry_f = z
    for j in range(SEGMENTS):
        carry_f = jnp.where(sub == j, cf, carry_f)
        cf = hf[j:j + 1, :] + af[j:j + 1, :] * cf
    carry_b = z
    for j in reversed(range(SEGMENTS)):
        carry_b = jnp.where(sub == j, cbk, carry_b)
        cbk = hb[j:j + 1, :] + ab[j:j + 1, :] * cbk
    fin_ref[0:1, :] = cf
    fin_ref[1:2, :] = cbk

    def pass2(t, carry):
        hf, hb = carry
        sb = seg_len - 1 - t
        hf = a_ref[0, rows(t), :] * hf + b_ref[0, rows(t), :]
        h_ref[0, rows(t), :] = hf
        hb = a_ref[1, rows(sb), :] * hb + b_ref[1, rows(sb), :]
        h_ref[1, rows(sb), :] = hb
        return hf, hb

    lax.fori_loop(0, seg_len, pass2, (carry_f, carry_b), unroll=SCAN_UNROLL)

    for t in range(n // tile):
        r0 = t * tile
        seg = r0 // seg_len
        s0 = r0 - seg * seg_len
        src = pl.ds(s0 * SEGMENTS + seg, tile, stride=SEGMENTS)
        hsum = h_ref[0, src, :] + h_ref[1, src, :]
        y_ref[pl.ds(r0, tile), :] = hsum * jax.nn.gelu(ug_ref[pl.ds(r0, tile), :])


def _lru_call(ux, ug, lp, h0):
    b, n, _ = ux.shape
    nblk = D_LRU // LANES
    has_h0 = h0 is not None
    blk = pl.BlockSpec((None, n, LANES), lambda i, j: (i, 0, j))
    in_specs = [
        blk, blk,
        pl.BlockSpec((CONV_W, LANES), lambda i, j: (0, j)),
        pl.BlockSpec((1, LANES), lambda i, j: (0, j)),
        pl.BlockSpec((None, LANES, 4 * LANES), lambda i, j: (j, 0, 0)),
        pl.BlockSpec((None, 1, 4 * LANES), lambda i, j: (j, 0, 0)),
        pl.BlockSpec((2, LANES), lambda i, j: (0, j)),
    ]
    args = [ux, ug, lp["conv_w"], lp["conv_b"], lp["w_gates"], lp["b_gates"], lp["lam"]]
    if has_h0:
        in_specs.append(pl.BlockSpec((None, 2, LANES), lambda i, j: (i, 0, j)))
        args.append(h0)
    return pl.pallas_call(
        functools.partial(_lru_kernel, has_h0),
        grid=(b, nblk),
        in_specs=in_specs,
        out_specs=[blk, pl.BlockSpec((None, 2, LANES), lambda i, j: (i, 0, j))],
        out_shape=[jax.ShapeDtypeStruct((b, n, D_LRU), F32),
                   jax.ShapeDtypeStruct((b, 2, D_LRU), F32)],
        scratch_shapes=[
            pltpu.VMEM((n + 2 * PAD_ROWS, LANES), F32),
            pltpu.VMEM((2, n, LANES), F32),
            pltpu.VMEM((2, n, LANES), F32),
            pltpu.VMEM((2, n, LANES), F32),
        ],
        compiler_params=_cparams(("arbitrary", "arbitrary")),
        name="lru",
    )(*args)


def _lru_params(conv_w, conv_b, wa, ba, wi, bi, lam):
    nblk = D_LRU // LANES
    hpb = LANES // LRU_HEAD_DIM

    def blockdiag(w):
        w = w.reshape(nblk, hpb, LRU_HEAD_DIM, LRU_HEAD_DIM)
        eye = jnp.eye(hpb, dtype=w.dtype)
        return jnp.einsum("bhij,hg->bhigj", w, eye).reshape(nblk, LANES, LANES)

    w_gates = jnp.concatenate([blockdiag(wa[0]), blockdiag(wi[0]), blockdiag(wa[1]), blockdiag(wi[1])], axis=-1)
    b_gates = jnp.stack([ba[0], bi[0], ba[1], bi[1]], axis=0)
    b_gates = b_gates.reshape(4, nblk, LANES).transpose(1, 0, 2).reshape(nblk, 1, 4 * LANES)
    return {"conv_w": conv_w, "conv_b": conv_b.reshape(1, D_LRU), "w_gates": w_gates,
            "b_gates": b_gates, "lam": lam}


def _mixout_kernel(x_ref, yf_ref, yr_ref, w_ref, g1_ref, n2_ref, sh_ref, sc_ref, wr_ref,
                   xo_ref, h2_ref, lg_ref, wbf_ref):
    @pl.when((pl.program_id(0) == 0) & (pl.program_id(1) == 0))
    def _cast_w():
        wbf_ref[...] = w_ref[...].astype(BF16)

    mix = jnp.dot(yf_ref[...].astype(BF16), wbf_ref[:D_FOURIER, :], preferred_element_type=F32)
    mix = mix + jnp.dot(yr_ref[...].astype(BF16), wbf_ref[D_FOURIER:, :], preferred_element_type=F32)
    x = x_ref[...] + g1_ref[...] * mix
    xo_ref[...] = x
    h2 = _rms_mod(x, n2_ref[...], sh_ref[...], sc_ref[...])
    tm = h2.shape[0]
    for s in range(TOKEN_ROWS):
        h2_ref[pl.ds(s, tm, stride=TOKEN_ROWS), :] = h2[:, s * LANES:(s + 1) * LANES]
    lg_ref[...] = lax.dot_general(wr_ref[...], h2, (((1,), (1,)), ((), ())),
                                  preferred_element_type=F32, precision=lax.Precision.HIGHEST)


def _mixout_call(x, yf, yr, w_out, g1, n2g, sh2, sc2, w_router_t, layer=0):
    b, n, d = x.shape
    tm = min(ROW_TILE, n)
    row = pl.BlockSpec((None, tm, d), lambda i, j: (i, j, 0))
    half = pl.BlockSpec((None, tm, D_FOURIER), lambda i, j: (i, j, 0))
    vec_b = pl.BlockSpec((None, 1, d), lambda i, j: (i, 0, 0))
    return pl.pallas_call(
        _mixout_kernel,
        grid=(b, n // tm),
        in_specs=[row, half, half,
                  _layer_weight_spec(w_out, layer),
                  vec_b,
                  pl.BlockSpec((1, d), lambda i, j: (0, 0)),
                  vec_b, vec_b,
                  pl.BlockSpec((N_EXPERTS, d), lambda i, j: (0, 0))],
        out_specs=[row,
                   pl.BlockSpec((tm * TOKEN_ROWS, LANES), lambda i, j: (i * (n // tm) + j, 0)),
                   pl.BlockSpec((None, N_EXPERTS, tm), lambda i, j: (i, 0, j))],
        out_shape=[jax.ShapeDtypeStruct((b, n, d), F32),
                   jax.ShapeDtypeStruct((b * n * TOKEN_ROWS, LANES), F32),
                   jax.ShapeDtypeStruct((b, N_EXPERTS, n), F32)],
        scratch_shapes=[pltpu.VMEM((d, d), BF16)],
        compiler_params=_cparams(("arbitrary", "arbitrary")),
        name="mixout",
    )(x, yf, yr, w_out, g1, n2g.reshape(1, d), sh2, sc2, w_router_t)


def _lane_prefix_sum(x):
    n = x.shape[-1]
    lane = lax.broadcasted_iota(jnp.int32, x.shape, x.ndim - 1)
    s = 1
    while s < n:
        x = x + jnp.where(lane >= s, pltpu.roll(x, s, x.ndim - 1), 0)
        s *= 2
    return x


QUARTER_STEPS = 16
BRACKET = 2.0 ** -6
TINY = 1e-37


def _route_kernel(cap, lg_ref, key_ref, csum_ref, keyt_ref, afft_ref):
    n = lg_ref.shape[1]
    lg = lg_ref[...]
    m = jnp.max(lg, axis=0, keepdims=True)
    ex = jnp.exp(lg - m)
    aff = ex / jnp.sum(ex, axis=0, keepdims=True)
    bits = pltpu.bitcast(aff, jnp.int32)

    def enough(pred):
        return jnp.sum(pred.astype(jnp.int32), axis=1, keepdims=True) >= cap

    def bis(_, carry):
        lo, hi = carry
        q = (hi - lo + 3) >> 2
        m1 = jnp.minimum(lo + q, hi)
        m2 = jnp.minimum(lo + 2 * q, hi)
        m3 = jnp.minimum(lo + 3 * q, hi)
        ok1, ok2, ok3 = enough(bits >= m1), enough(bits >= m2), enough(bits >= m3)
        new_lo = jnp.where(ok3, m3, jnp.where(ok2, m2, jnp.where(ok1, m1, lo)))
        new_hi = jnp.where(ok1, jnp.where(ok2, jnp.where(ok3, hi, m3 - 1), m2 - 1), m1 - 1)
        return new_lo, new_hi

    lo0 = jnp.zeros((N_EXPERTS, 1), jnp.int32)
    hi0 = jnp.full((N_EXPERTS, 1), 0x7F800000, jnp.int32)
    thr, _ = lax.fori_loop(0, QUARTER_STEPS + 1, bis, (lo0, hi0))

    thr_f = pltpu.bitcast(thr, F32)
    flo0 = thr_f * (1.0 - BRACKET)
    fhi0 = jnp.maximum(thr_f * (1.0 + BRACKET), jnp.float32(TINY))

    def fbis(_, carry):
        lo, hi = carry
        w = hi - lo
        m1, m2, m3 = lo + 0.25 * w, lo + 0.5 * w, lo + 0.75 * w
        ok1, ok2, ok3 = enough(aff >= m1), enough(aff >= m2), enough(aff >= m3)
        new_lo = jnp.where(ok3, m3, jnp.where(ok2, m2, jnp.where(ok1, m1, lo)))
        new_hi = jnp.where(ok1, jnp.where(ok2, jnp.where(ok3, hi, m3), m2), m1)
        return new_lo, new_hi

    flo, fhi = lax.fori_loop(0, QUARTER_STEPS, fbis, (flo0, fhi0))
    gt = aff >= fhi
    eq = (aff >= flo) & (aff < fhi)
    need = cap - jnp.sum(gt.astype(jnp.int32), axis=1, keepdims=True)
    eq_rank = _lane_prefix_sum(eq.astype(jnp.int32)) - eq.astype(jnp.int32)
    sel = gt | (eq & (eq_rank < need))
    csum = _lane_prefix_sum(sel.astype(jnp.int32))
    key = jnp.where(sel, csum - 1, -1)
    key_ref[...] = key
    csum_ref[...] = csum

    pad = jnp.zeros((LANES - N_EXPERTS, n), F32)
    keyt_ref[...] = jnp.concatenate([key.astype(F32), pad - 1.0], axis=0).T
    afft_ref[...] = jnp.concatenate([aff, pad], axis=0).T


def _route_call(logits_t, cap):
    b, _, n = logits_t.shape
    en = pl.BlockSpec((None, N_EXPERTS, n), lambda i: (i, 0, 0))
    tm = pl.BlockSpec((None, n, LANES), lambda i: (i, 0, 0))
    return pl.pallas_call(
        functools.partial(_route_kernel, cap),
        grid=(b,),
        in_specs=[en],
        out_specs=[en, en, tm, tm],
        out_shape=[jax.ShapeDtypeStruct((b, N_EXPERTS, n), jnp.int32),
                   jax.ShapeDtypeStruct((b, N_EXPERTS, n), jnp.int32),
                   jax.ShapeDtypeStruct((b, n, LANES), F32),
                   jax.ShapeDtypeStruct((b, n, LANES), F32)],
        compiler_params=_cparams(("arbitrary",)),
        name="route",
    )(logits_t)


COMPACT_TILE = 64


def _compact_kernel(cap, klo_ref, khi_ref, key_ref, idx_ref):
    b = pl.program_id(0)
    nblk = key_ref.shape[1]
    rt = min(COMPACT_TILE, cap)
    ntile = cap // rt
    width = min(LANES, cap)
    per_store = width // rt
    lane = lax.broadcasted_iota(jnp.int32, (1, LANES), 1)
    sub = lax.broadcasted_iota(jnp.int32, (rt, 1), 0)
    base = b * (nblk * LANES)
    zero = jnp.zeros((rt, LANES), F32)
    for e in range(N_EXPERTS):
        row = jnp.zeros((1, LANES), F32)
        for t in range(ntile):
            s = (b * N_EXPERTS + e) * ntile + t
            rr = t * rt + sub

            def block(k, ids, e=e, rr=rr):
                p = key_ref[e, pl.ds(k, 1), :] == rr
                return ids + jnp.where(p, (base + k * LANES + lane).astype(F32), 0.0)

            ids = lax.fori_loop(klo_ref[s], khi_ref[s], block, zero)
            col = jnp.sum(ids, axis=1, keepdims=True)
            diag = (sub + (t % per_store) * rt) == lane
            row = row + jnp.sum(jnp.where(diag, col, 0.0), axis=0, keepdims=True)
            if (t + 1) % per_store == 0:
                g = t // per_store
                idx_ref[e:e + 1, g * width:(g + 1) * width] = row[:, :width].astype(jnp.int32)
                row = jnp.zeros((1, LANES), F32)


def _compact_call(key, csum, cap):
    b, _, n = key.shape
    nblk = n // LANES
    rt = min(COMPACT_TILE, cap)
    ntile = cap // rt
    cb_incl = csum[:, :, LANES - 1::LANES]
    cb_excl = jnp.concatenate([jnp.zeros_like(cb_incl[..., :1]), cb_incl[..., :-1]], axis=-1)
    t0 = (jnp.arange(ntile, dtype=jnp.int32) * rt)[:, None]
    klo = jnp.sum((cb_incl[:, :, None, :] <= t0).astype(jnp.int32), axis=-1)
    khi = jnp.sum((cb_excl[:, :, None, :] < t0 + rt).astype(jnp.int32), axis=-1)
    return pl.pallas_call(
        functools.partial(_compact_kernel, cap),
        grid_spec=pltpu.PrefetchScalarGridSpec(
            num_scalar_prefetch=2, grid=(b,),
            in_specs=[pl.BlockSpec((None, N_EXPERTS, nblk, LANES), lambda i, *_: (i, 0, 0, 0))],
            out_specs=pl.BlockSpec((None, N_EXPERTS, cap), lambda i, *_: (i, 0, 0))),
        out_shape=jax.ShapeDtypeStruct((b, N_EXPERTS, cap), jnp.int32),
        compiler_params=_cparams(("arbitrary",)),
        name="compact",
    )(klo.reshape(-1), khi.reshape(-1), key.reshape(b, N_EXPERTS, nblk, LANES))


FF_TILE = 256


FF_STEPS = D_FF // FF_TILE
ROW_ALIGN = 32


def _round_up(v, m):
    return -(-v // m) * m


def _ffn_layout(row_counts):
    layout, off = [], 0
    for m in row_counts:
        padded = _round_up(m, FF_STEPS * SUBLANES)
        layout.append((m, padded, off))
        off = _round_up(off + padded, ROW_ALIGN)
    out_rows = _round_up(layout[-1][2] + layout[-1][0], ROW_ALIGN)
    scratch_rows = _round_up(layout[-1][2] + layout[-1][1], SUBLANES)
    return tuple(layout), out_rows, max(scratch_rows, out_rows)


def _ffn_kernel(layout, *refs):
    ns = len(layout)
    idx_refs = refs[:ns]
    tabs = refs[ns:2 * ns]
    wg_ref, wu_ref, wd_ref, o_ref, xg32_ref, xg_ref, acc_ref, sem = refs[2 * ns:]
    e = pl.program_id(0)
    f = pl.program_id(1)
    n_e = pl.num_programs(0)
    slot = e % 2

    def issue_chunk(expert, dst_slot, step):
        for s, (_, padded, off) in enumerate(layout):
            chunk = padded // FF_STEPS
            src0 = expert * padded + step * chunk
            dst0 = off + step * chunk

            def body(g, carry, s=s, src0=src0, dst0=dst0):
                src = src0 + g * SUBLANES
                dst = pl.multiple_of(dst0 + g * SUBLANES, SUBLANES)
                for j in range(SUBLANES):
                    tok = idx_refs[s][src + j]
                    pltpu.make_async_copy(
                        tabs[s].at[pl.ds(pl.multiple_of(tok * TOKEN_ROWS, TOKEN_ROWS), TOKEN_ROWS), :],
                        xg32_ref.at[dst_slot, pl.ds(pl.multiple_of((dst + j) * TOKEN_ROWS, TOKEN_ROWS), TOKEN_ROWS), :],
                        sem.at[dst_slot]).start()
                return carry
            lax.fori_loop(0, chunk // SUBLANES, body, 0)

    @pl.when((e == 0) & (f == 0))
    def _first_expert():
        for step in range(FF_STEPS):
            issue_chunk(0, 0, step)

    @pl.when(f == 0)
    def _start_expert():
        for s, (m, padded, off) in enumerate(layout):
            pltpu.make_async_copy(tabs[s].at[pl.ds(0, padded * TOKEN_ROWS), :],
                                  xg32_ref.at[slot, pl.ds(off * TOKEN_ROWS, padded * TOKEN_ROWS), :],
                                  sem.at[slot]).wait()
        pack = 2 * SUBLANES
        for m, _, off in layout:
            def unpack(g, carry, off=off):
                t0 = pl.multiple_of(off + g * pack, pack)
                for c in range(TOKEN_ROWS):
                    v = xg32_ref[slot, pl.ds(t0 * TOKEN_ROWS + c, pack, stride=TOKEN_ROWS), :]
                    xg_ref[pl.ds(t0, pack), c * LANES:(c + 1) * LANES] = v.astype(BF16)
                return carry
            lax.fori_loop(0, m // pack, unpack, 0)
            acc_ref[pl.ds(off, m), :] = jnp.zeros((m, acc_ref.shape[1]), F32)

    nxt = jnp.minimum(e + 1, n_e - 1)
    other = 1 - slot
    copies = []
    for s, (_, padded, off) in enumerate(layout):
        chunk = padded // FF_STEPS
        for r in range(chunk):
            copies.append((s, nxt * padded + f * chunk + r, off + f * chunk, r))
    tiles = [(off + i * min(ROW_TILE, m), min(ROW_TILE, m))
             for m, _, off in layout for i in range(m // min(ROW_TILE, m))]
    share = -(-len(copies) // len(tiles))

    wg = wg_ref[...].astype(BF16)
    wu = wu_ref[...].astype(BF16)
    wd = wd_ref[...].astype(BF16)
    for ti, (r0, mt) in enumerate(tiles):
        for s, src, dst0, r in copies[ti * share:(ti + 1) * share]:
            tok = idx_refs[s][src]
            dst = pl.multiple_of(dst0 * TOKEN_ROWS, SUBLANES * TOKEN_ROWS) + r * TOKEN_ROWS
            pltpu.make_async_copy(
                tabs[s].at[pl.ds(pl.multiple_of(tok * TOKEN_ROWS, TOKEN_ROWS), TOKEN_ROWS), :],
                xg32_ref.at[other, pl.ds(pl.multiple_of(dst, TOKEN_ROWS), TOKEN_ROWS), :],
                sem.at[other]).start()
        rows = pl.ds(r0, mt)
        xg = xg_ref[rows, :]
        g = jnp.dot(xg, wg, preferred_element_type=F32)
        u = jnp.dot(xg, wu, preferred_element_type=F32)
        hid = (g * jax.nn.sigmoid(g) * u).astype(BF16)
        acc_ref[rows, :] += jnp.dot(hid, wd, preferred_element_type=F32)

    @pl.when((e == n_e - 1) & (f == FF_STEPS - 1))
    def _drain():
        for s, (_, padded, off) in enumerate(layout):
            pltpu.make_async_copy(tabs[s].at[pl.ds(0, padded * TOKEN_ROWS), :],
                                  xg32_ref.at[other, pl.ds(off * TOKEN_ROWS, padded * TOKEN_ROWS), :],
                                  sem.at[other]).wait()

    @pl.when(f == FF_STEPS - 1)
    def _finish():
        end = 0
        for m, _, off in layout:
            if off > end:
                o_ref[pl.ds(end, off - end), :] = jnp.zeros((off - end, o_ref.shape[1]), o_ref.dtype)
            o_ref[pl.ds(off, m), :] = acc_ref[pl.ds(off, m), :].astype(o_ref.dtype)
            end = off + m
        if o_ref.shape[0] > end:
            o_ref[pl.ds(end, o_ref.shape[0] - end), :] = jnp.zeros((o_ref.shape[0] - end, o_ref.shape[1]), o_ref.dtype)


def _ffn_call(layer, row_counts, idxs, tables, w_gate, w_up, w_down):
    ns = len(tables)
    n_e = w_gate.shape[1]
    d = w_gate.shape[2]
    layout, out_rows, scratch_rows = _ffn_layout(row_counts)
    grid_spec = pltpu.PrefetchScalarGridSpec(
        num_scalar_prefetch=ns,
        grid=(n_e, FF_STEPS),
        in_specs=[pl.BlockSpec(memory_space=pl.ANY)] * ns
        + [pl.BlockSpec((None, None, d, FF_TILE), lambda e, f, *_: (layer, e, 0, f)),
           pl.BlockSpec((None, None, d, FF_TILE), lambda e, f, *_: (layer, e, 0, f)),
           pl.BlockSpec((None, None, FF_TILE, d), lambda e, f, *_: (layer, e, f, 0))],
        out_specs=pl.BlockSpec((None, out_rows, d), lambda e, f, *_: (e, 0, 0)),
        scratch_shapes=[pltpu.VMEM((2, scratch_rows * TOKEN_ROWS, LANES), F32), pltpu.VMEM((scratch_rows, d), BF16),
                        pltpu.VMEM((scratch_rows, d), F32), pltpu.SemaphoreType.DMA((2,))],
    )
    return pl.pallas_call(
        functools.partial(_ffn_kernel, layout),
        grid_spec=grid_spec,
        out_shape=jax.ShapeDtypeStruct((n_e, out_rows, d), BF16),
        compiler_params=_cparams(("arbitrary", "arbitrary")),
        name="ffn",
    )(*idxs, *tables, w_gate, w_up, w_down)


COMBINE_TOKENS = ROW_TILE
COMBINE_WINDOW = 256
ROW_PACK = 16


def _combine_kernel(final, win, start_ref, spill_ref, *refs):
    if final:
        x_ref, kt_ref, at_ref, y_ref, g2_ref, fg_ref, o_ref, acc_ref = refs
    else:
        x_ref, kt_ref, at_ref, y_ref, g2_ref, o_ref, acc_ref = refs
    cap = y_ref.shape[1]
    windowed = win < cap
    step = (pl.program_id(0) * pl.num_programs(1) + pl.program_id(1)) * N_EXPERTS
    kt = kt_ref[...]
    at = at_ref[...]
    lane = lax.broadcasted_iota(jnp.int32, (1, win), 1).astype(F32)

    acc = jnp.zeros(x_ref.shape, F32)
    for e in range(N_EXPERTS):
        if windowed:
            s0 = pl.multiple_of(start_ref[step + e], ROW_PACK)
            q = (kt[:, e:e + 1] == lane + s0.astype(F32)).astype(BF16)
            part = jnp.dot(q, y_ref[e, pl.ds(s0, win), :], preferred_element_type=F32)
        else:
            q = (kt[:, e:e + 1] == lane).astype(BF16)
            part = jnp.dot(q, y_ref[e], preferred_element_type=F32)
        acc = acc + at[:, e:e + 1] * part

    if windowed:
        acc_ref[...] = acc
        for e in range(N_EXPERTS):
            @pl.when(spill_ref[step + e] != 0)
            def _rest(e=e):
                done = (start_ref[step + e] + win).astype(F32)
                ke = kt_ref[:, e:e + 1]
                ke = jnp.where(ke >= done, ke, -1.0)
                q = (ke == lane + float(cap - win)).astype(BF16)
                acc_ref[...] += at_ref[:, e:e + 1] * jnp.dot(q, y_ref[e, cap - win:, :],
                                                             preferred_element_type=F32)
        acc = acc_ref[...]

    x = x_ref[...] + g2_ref[...] * acc
    if final:
        x = x * lax.rsqrt(jnp.mean(x * x, axis=-1, keepdims=True) + EPS) * fg_ref[...]
    o_ref[...] = x


def _combine_call(x, keyt, afft, csum, yexp, first_row, cap, g2, final_g):
    b, n, d = x.shape
    tc = min(COMBINE_TOKENS, n)
    win = min(COMBINE_WINDOW, cap)
    assert cap in (win, 2 * win)
    final = final_g is not None
    blk0 = first_row // cap
    upto = csum[:, :, tc - 1::tc]
    before = jnp.concatenate([jnp.zeros_like(upto[..., :1]), upto[..., :-1]], axis=-1)
    start = jnp.clip((before // ROW_PACK) * ROW_PACK, 0, cap - win)
    spill = (upto > start + win).astype(jnp.int32)
    start = start.transpose(0, 2, 1).reshape(-1)
    spill = spill.transpose(0, 2, 1).reshape(-1)
    row = pl.BlockSpec((None, tc, d), lambda i, j, *_: (i, j, 0))
    tok = pl.BlockSpec((None, tc, LANES), lambda i, j, *_: (i, j, 0))
    in_specs = [row, tok, tok,
                pl.BlockSpec((N_EXPERTS, cap, d), lambda i, j, *_: (0, blk0 + i, 0)),
                pl.BlockSpec((None, 1, d), lambda i, j, *_: (i, 0, 0))]
    args = [x, keyt, afft, yexp, g2]
    if final:
        in_specs.append(pl.BlockSpec((1, d), lambda i, j, *_: (0, 0)))
        args.append(final_g.reshape(1, d))
    return pl.pallas_call(
        functools.partial(_combine_kernel, final, win),
        grid_spec=pltpu.PrefetchScalarGridSpec(
            num_scalar_prefetch=2, grid=(b, n // tc), in_specs=in_specs, out_specs=row,
            scratch_shapes=[pltpu.VMEM((tc, d), F32)]),
        out_shape=jax.ShapeDtypeStruct((b, n, d), F32),
        compiler_params=_cparams(("arbitrary", "arbitrary")),
        name="combine",
    )(start, spill, *args)


def _grid_pos_embed(rows, d):
    rr, cc = np.meshgrid(np.arange(rows, dtype=np.float64), np.arange(GRID_W, dtype=np.float64), indexing="ij")
    quarter = d // 4
    omega = 1.0 / (POS_BASE ** (np.arange(quarter, dtype=np.float64) / quarter))

    def enc(p):
        ang = p.reshape(-1)[:, None] * omega
        return np.concatenate([np.sin(ang), np.cos(ang)], axis=-1)

    return np.concatenate([enc(rr), enc(cc)], axis=-1).astype(np.float32)


def _moe(layer, streams, w_gate, w_up, w_down):
    d = streams[0]["x"].shape[-1]
    caps = [EC_CAPACITY * s["x"].shape[1] // N_EXPERTS for s in streams]
    row_counts = [s["x"].shape[0] * cap for s, cap in zip(streams, caps)]
    layout, _, _ = _ffn_layout(row_counts)
    idxs, routes = [], []
    for s, cap, (m, padded, off) in zip(streams, caps, layout):
        key, csum, keyt, afft = _route_call(s["logits_t"], cap)
        idx = _compact_call(key, csum, cap)
        idx = idx.transpose(1, 0, 2).reshape(N_EXPERTS, m)
        idxs.append(jnp.pad(idx, ((0, 0), (0, padded - m))).reshape(N_EXPERTS * padded))
        routes.append((keyt, afft, csum))
    tables = [s["h2"] for s in streams]
    yexp = _ffn_call(layer, row_counts, idxs, tables, w_gate, w_up, w_down)
    return [_combine_call(s["x"], keyt, afft, csum, yexp, off, cap, s["g2"], s["final_g"])
            for s, (keyt, afft, csum), cap, (_, _, off) in zip(streams, routes, caps, layout)]


def kernel(x, c, ctx, c_ctx, w_mod, b_mod, norm1_g, norm2_g, w_in, conv_w, conv_b, lru_wa, lru_ba,
           lru_wi, lru_bi, lru_lambda, w_out, w_router, w_gate, w_up, w_down, final_g):
    bsz, n, d = x.shape
    depth = w_mod.shape[0]
    pos = jnp.asarray(_grid_pos_embed(n // GRID_W, d))

    crows = jnp.zeros((SUBLANES, d), F32).at[:bsz].set(c).at[bsz].set(c_ctx)
    mod = _mod_call(crows, w_mod, b_mod)

    for l in range(depth):
        last = l == depth - 1
        mx = mod[l, :bsz].reshape(bsz, 1, N_MOD, d)
        sh1, sc1, g1, sh2, sc2, g2 = [mx[:, :, i, :] for i in range(N_MOD)]
        mc = jnp.broadcast_to(mod[l, bsz].reshape(1, 1, N_MOD, d), (bsz, 1, N_MOD, d))
        csh1, csc1, cg1, csh2, csc2, cg2 = [mc[:, :, i, :] for i in range(N_MOD)]
        w_router_t = w_router[l].T
        lp = _lru_params(conv_w[l], conv_b[l], lru_wa[l], lru_ba[l], lru_wi[l], lru_bi[l], lru_lambda[l])

        cuf, cux, cug = _proj_call(ctx, None, norm1_g[l], csh1, csc1, w_in, l)
        cyr, ctx_states = _lru_call(cux, cug, lp, None)
        streams = []
        if not last:
            cyf = _fourier_call(cuf)
            ctx, ch2, clg = _mixout_call(ctx, cyf, cyr, w_out, cg1, norm2_g[l], csh2, csc2, w_router_t, l)
            streams.append(dict(x=ctx, h2=ch2, logits_t=clg, g2=cg2, final_g=None))

        if l == 0:
            uf, ux, ug, x = _proj_call(x, pos, norm1_g[l], sh1, sc1, w_in, l)
        else:
            uf, ux, ug = _proj_call(x, None, norm1_g[l], sh1, sc1, w_in, l)
        yf = _fourier_call(uf)
        yr, _ = _lru_call(ux, ug, lp, ctx_states)
        x, h2, lg = _mixout_call(x, yf, yr, w_out, g1, norm2_g[l], sh2, sc2, w_router_t, l)
        streams.insert(0, dict(x=x, h2=h2, logits_t=lg, g2=g2, final_g=final_g if last else None))
        outs = _moe(l, streams, w_gate, w_up, w_down)
        x = outs[0]
        if not last:
            ctx = outs[1]
    return x
```
